```python
import math
import jax, jax.numpy as jnp
from jax import lax
import numpy as np

D_MODEL = 2048
BATCH = 1
SEQ = 16384
DEPTH = 1

CHUNK = 64
Q_BLOCK = 128
D_MIX = D_MODEL
D_ATTN = D_MIX // 2
D_CONV = D_MIX - D_ATTN
V_HEAD_DIM = 128
N_DIFF_HEADS = D_ATTN // V_HEAD_DIM
QK_HEAD_DIM = V_HEAD_DIM // 2
CONV_WIDTH = 3
N_GROUPS = 4
EXPERTS_PER_GROUP = 8
N_EXPERTS = N_GROUPS * EXPERTS_PER_GROUP
TOP_K_INNER = 2
D_EXPERT = 512
MOE_BLOCK = 128
NORM_EPS = 1e-6
SUBLN_EPS = 1e-5
D_IN_PROJ = 3 * D_ATTN + 3 * D_CONV

kernel_name = "hybrid_diffattn_shortconv_hmoe_layer"


def rmsnorm(x, g, eps=NORM_EPS):
    xf = x.astype(jnp.float32)
    y = xf * lax.rsqrt(jnp.mean(xf * xf, axis=-1, keepdims=True) + eps)
    return (y * g.astype(jnp.float32)).astype(x.dtype)


def alibi_slopes(n_heads):
    start = 2.0 ** (-8.0 / n_heads)
    return jnp.asarray(start ** np.arange(1, n_heads + 1), dtype=jnp.float32)


def diff_attention(q, k, v, lam, lam_init, subln_w):
    b, s = q.shape[0], q.shape[1]
    nb = s // Q_BLOCK
    q = q * jnp.asarray(QK_HEAD_DIM ** -0.5, dtype=q.dtype)
    qb = jnp.moveaxis(q.reshape(b, nb, Q_BLOCK, N_DIFF_HEADS, 2, QK_HEAD_DIM), 1, 0)
    starts = jnp.arange(nb, dtype=jnp.int32) * Q_BLOCK
    slopes = alibi_slopes(N_DIFF_HEADS)
    kpos = jnp.arange(s, dtype=jnp.int32)

    def attend(args):
        q_blk, start = args
        sc = jnp.einsum('bqhmd,bkhmd->bmhqk', q_blk, k).astype(jnp.float32)
        qpos = start + jnp.arange(Q_BLOCK, dtype=jnp.int32)
        dist = jnp.abs(qpos[:, None] - kpos[None, :]).astype(jnp.float32)
        allowed = (kpos // CHUNK)[None, :] <= (qpos // CHUNK)[:, None]
        sc = jnp.where(allowed, sc - slopes[:, None, None] * dist, -jnp.inf)
        p = jax.nn.softmax(sc, axis=-1)
        a = p[:, 0] - lam * p[:, 1]
        return jnp.einsum('bhqk,bkhd->bqhd', a.astype(v.dtype), v)

    o = lax.map(attend, (qb, starts))
    o = jnp.moveaxis(o, 0, 1).reshape(b, s, N_DIFF_HEADS, V_HEAD_DIM)
    o = rmsnorm(o, subln_w, SUBLN_EPS) * jnp.asarray(1.0 - lam_init, dtype=o.dtype)
    return o.reshape(b, s, N_DIFF_HEADS * V_HEAD_DIM)


def short_conv(bg, cg, u, w):
    z = cg * u
    conv = lax.conv_general_dilated(
        z, w[:, None, :].astype(z.dtype), window_strides=(1,),
        padding=[(CONV_WIDTH - 1, 0)], dimension_numbers=('NWC', 'WIO', 'NWC'),
        feature_group_count=z.shape[-1])
    return bg * conv


def hier_moe(h, w_rg, b_rg, w_re, b_re, w_gate, w_up, w_down):
    b, s, d = h.shape
    t = h.reshape(-1, d)
    g_prob = jax.nn.softmax((t @ w_rg).astype(jnp.float32) + b_rg.astype(jnp.float32), axis=-1)
    g_w, g_idx = lax.top_k(g_prob, 1)
    e_logits = jnp.einsum('nd,dge->nge', t, w_re).astype(jnp.float32) + b_re.astype(jnp.float32)
    e_logits = jnp.take_along_axis(e_logits, g_idx[:, :, None], axis=1)[:, 0]
    e_w, e_idx = lax.top_k(jax.nn.softmax(e_logits, axis=-1), TOP_K_INNER)
    e_w = e_w / jnp.sum(e_w, axis=-1, keepdims=True)
    gate = g_w * e_w
    flat_idx = g_idx * EXPERTS_PER_GROUP + e_idx
    combine = jnp.sum(jax.nn.one_hot(flat_idx, N_EXPERTS, dtype=jnp.float32) * gate[..., None], axis=1)
    tb = t.reshape(-1, MOE_BLOCK, d)
    cb = combine.reshape(-1, MOE_BLOCK, N_EXPERTS).astype(t.dtype)

    def expert_block(args):
        tk, ck = args
        hg = jnp.einsum('td,edf->tef', tk, w_gate)
        hu = jnp.einsum('td,edf->tef', tk, w_up)
        a = jax.nn.silu(hg) * hu * ck[:, :, None]
        return jnp.einsum('tef,efd->td', a, w_down)

    return lax.map(expert_block, (tb, cb)).reshape(b, s, d)


def setup_inputs(seed: int = 0) -> dict:
    key = jax.random.key(seed)
    ks = jax.random.split(key, 24)
    f32 = jnp.float32
    nrm = lambda k, shape, scale: jax.random.normal(k, shape, f32) * scale
    gain = lambda k, shape: 1.0 + 0.05 * jax.random.normal(k, shape, f32)
    L = DEPTH
    return {
        "x": nrm(ks[0], (BATCH, SEQ, D_MODEL), 1.0),
        "c": nrm(ks[1], (BATCH, D_MODEL), 1.0),
        "w_ada": nrm(ks[2], (L, D_MODEL, 6 * D_MODEL), 0.5 * D_MODEL ** -0.5),
        "b_ada": nrm(ks[3], (L, 6 * D_MODEL), 0.02),
        "g_pre1": gain(ks[4], (L, D_MODEL)),
        "w_in": nrm(ks[5], (L, D_MODEL, D_IN_PROJ), D_MODEL ** -0.5),
        "lam_q1": nrm(ks[6], (L, QK_HEAD_DIM), 0.1),
        "lam_k1": nrm(ks[7], (L, QK_HEAD_DIM), 0.1),
        "lam_q2": nrm(ks[8], (L, QK_HEAD_DIM), 0.1),
        "lam_k2": nrm(ks[9], (L, QK_HEAD_DIM), 0.1),
        "g_subln": gain(ks[10], (L, V_HEAD_DIM)),
        "w_conv": nrm(ks[11], (L, CONV_WIDTH, D_CONV), CONV_WIDTH ** -0.5),
        "w_out": nrm(ks[12], (L, D_MIX, D_MODEL), D_MIX ** -0.5),
        "g_post1": gain(ks[13], (L, D_MODEL)),
        "g_pre2": gain(ks[14], (L, D_MODEL)),
        "w_router_g": nrm(ks[15], (L, D_MODEL, N_GROUPS), D_MODEL ** -0.5),
        "b_router_g": nrm(ks[16], (L, N_GROUPS), 0.01),
        "w_router_e": nrm(ks[17], (L, D_MODEL, N_GROUPS, EXPERTS_PER_GROUP), D_MODEL ** -0.5),
        "b_router_e": nrm(ks[18], (L, N_GROUPS, EXPERTS_PER_GROUP), 0.01),
        "w_gate": nrm(ks[19], (L, N_EXPERTS, D_MODEL, D_EXPERT), D_MODEL ** -0.5),
        "w_up": nrm(ks[20], (L, N_EXPERTS, D_MODEL, D_EXPERT), D_MODEL ** -0.5),
        "w_down": nrm(ks[21], (L, N_EXPERTS, D_EXPERT, D_MODEL), D_EXPERT ** -0.5),
        "g_post2": gain(ks[22], (L, D_MODEL)),
    }


def reference(x, c, w_ada, b_ada, g_pre1, w_in, lam_q1, lam_k1, lam_q2, lam_k2, g_subln,
              w_conv, w_out, g_post1, g_pre2, w_router_g, b_router_g, w_router_e, b_router_e,
              w_gate, w_up, w_down, g_post2):
    b, s, _ = x.shape
    splits = [D_ATTN, 2 * D_ATTN, 3 * D_ATTN, 3 * D_ATTN + D_CONV, 3 * D_ATTN + 2 * D_CONV]
    for l in range(DEPTH):
        lam_init = 0.8 - 0.6 * math.exp(-0.3 * l)
        mod = jax.nn.silu(c) @ w_ada[l] + b_ada[l]
        sh1, sc1, gt1, sh2, sc2, gt2 = jnp.split(mod[:, None, :], 6, axis=-1)

        h = rmsnorm(x, g_pre1[l]) * (1.0 + sc1) + sh1
        proj = h @ w_in[l]
        q, k, v, bg, cg, u = jnp.split(proj, splits, axis=-1)
        q = q.reshape(b, s, N_DIFF_HEADS, 2, QK_HEAD_DIM)
        k = k.reshape(b, s, N_DIFF_HEADS, 2, QK_HEAD_DIM)
        v = v.reshape(b, s, N_DIFF_HEADS, V_HEAD_DIM)
        lam = (jnp.exp(jnp.sum(lam_q1[l] * lam_k1[l]).astype(jnp.float32))
               - jnp.exp(jnp.sum(lam_q2[l] * lam_k2[l]).astype(jnp.float32)) + lam_init)
        y_attn = diff_attention(q, k, v, lam, lam_init, g_subln[l])
        y_conv = short_conv(bg, cg, u, w_conv[l])
        y = jnp.concatenate([y_attn, y_conv], axis=-1) @ w_out[l]
        x = x + gt1 * rmsnorm(y, g_post1[l])

        h = rmsnorm(x, g_pre2[l]) * (1.0 + sc2) + sh2
        y = hier_moe(h, w_router_g[l], b_router_g[l], w_router_e[l], b_router_e[l],
                     w_gate[l], w_up[l], w_down[l])
        x = x + gt2 * rmsnorm(y, g_post2[l])
    return x
```

```python
import functools
import math

import jax
import jax.numpy as jnp
from jax import lax
from jax.experimental import pallas as pl
from jax.experimental.pallas import tpu as pltpu

F32 = jnp.float32
BF16 = jnp.bfloat16

LANES = 128
CHUNK = 64
N_HEADS = 8
QK_DIM = 64
V_DIM = 128
CONV_WIDTH = 3
N_GROUPS = 4
EXPERTS_PER_GROUP = 8
N_EXPERTS = N_GROUPS * EXPERTS_PER_GROUP
NORM_EPS = 1e-6
SUBLN_EPS = 1e-5
NEG_BIG = -1e30
VMEM_LIMIT = 56 * 1024 * 1024


def _rms(x, eps):
    return x * lax.rsqrt(jnp.mean(x * x, axis=-1, keepdims=True) + eps)


def _sigmoid(x):
    return 1.0 / (1.0 + jnp.exp(-x))


def _ada_kernel(c_ref, w_ref, b_ref, o_ref):
    c = c_ref[...]
    s = c * _sigmoid(c)
    o_ref[...] = jnp.sum(w_ref[...] * s, axis=0, keepdims=True) + b_ref[...]


def _ada(c_col, w_ada, b_ada, tn=1024):
    d, n = w_ada.shape
    return pl.pallas_call(
        _ada_kernel,
        out_shape=jax.ShapeDtypeStruct((1, n), F32),
        grid=(n // tn,),
        in_specs=[pl.BlockSpec((d, 1), lambda j: (0, 0)),
                  pl.BlockSpec((d, tn), lambda j: (0, j)),
                  pl.BlockSpec((1, tn), lambda j: (0, j))],
        out_specs=pl.BlockSpec((1, tn), lambda j: (0, j)),
        compiler_params=pltpu.CompilerParams(dimension_semantics=("arbitrary",),
                                             vmem_limit_bytes=VMEM_LIMIT),
        name="ada",
    )(c_col, w_ada, b_ada)


def _inproj_kernel(x_ref, g_ref, sc_ref, sh_ref, w_ref, o_ref, h_scr, *, cn):
    @pl.when(pl.program_id(1) == 0)
    def _():
        y = _rms(x_ref[...], NORM_EPS) * g_ref[...]
        h_scr[...] = (y * (1.0 + sc_ref[...]) + sh_ref[...]).astype(BF16)

    acc = jnp.dot(h_scr[...], w_ref[...], preferred_element_type=F32)
    for c in range(cn):
        o_ref[c] = acc[:, c * LANES:(c + 1) * LANES].astype(BF16)


def _inproj(x2, g, sc, sh, w_bf, tm=1024, cn=4):
    s, d = x2.shape
    n = w_bf.shape[1]
    nchunks = n // LANES
    row = lambda i, j: (0, 0)
    return pl.pallas_call(
        functools.partial(_inproj_kernel, cn=cn),
        out_shape=jax.ShapeDtypeStruct((nchunks, s, LANES), BF16),
        grid=(s // tm, nchunks // cn),
        in_specs=[pl.BlockSpec((tm, d), lambda i, j: (i, 0)),
                  pl.BlockSpec((1, d), row),
                  pl.BlockSpec((1, d), row),
                  pl.BlockSpec((1, d), row),
                  pl.BlockSpec((d, cn * LANES), lambda i, j: (0, j))],
        out_specs=pl.BlockSpec((cn, tm, LANES), lambda i, j: (j, i, 0)),
        scratch_shapes=[pltpu.VMEM((tm, d), BF16)],
        compiler_params=pltpu.CompilerParams(dimension_semantics=("arbitrary", "arbitrary"),
                                             vmem_limit_bytes=VMEM_LIMIT),
        name="inproj",
    )(x2, g, sc, sh, w_bf)


def _attn_kernel(slope_ref, lq1_ref, lk1_ref, lq2_ref, lk2_ref, q_ref, k_ref, v_ref, gsub_ref,
                 o_ref, kt_scr, *, t, lam_init):
    h = pl.program_id(0)
    qi = pl.program_id(1)
    nkt = kt_scr.shape[0]

    @pl.when(qi == 0)
    def _():
        def tr(c, carry):
            kc = k_ref[0, pl.ds(pl.multiple_of(c * t, t), t), :]
            kt_scr[c] = kc.astype(F32).T.astype(BF16)
            return carry
        lax.fori_loop(0, nkt, tr, 0)

    slope = slope_ref[h]
    q = q_ref[0].astype(F32) * (QK_DIM ** -0.5)
    lane = lax.broadcasted_iota(jnp.int32, q.shape, 1)
    qa = jnp.where(lane < QK_DIM, q, 0.0).astype(BF16)
    qb = jnp.where(lane >= QK_DIM, q, 0.0).astype(BF16)

    def update(qm, kt, v, bias, st):
        m, l, acc = st
        s = jnp.dot(qm, kt, preferred_element_type=F32) + bias
        m_new = jnp.maximum(m, jnp.max(s, axis=-1, keepdims=True))
        alpha = jnp.exp(m - m_new)
        p = jnp.exp(s - m_new)
        l = alpha * l + jnp.sum(p, axis=-1, keepdims=True)
        acc = alpha * acc + jnp.dot(p.astype(BF16), v, preferred_element_type=F32)
        return m_new, l, acc

    def off_diag(ki, carry):
        kt = kt_scr[ki]
        v = v_ref[0, pl.ds(pl.multiple_of(ki * t, t), t), :]
        krel = (ki - qi) * t + lax.broadcasted_iota(jnp.int32, (1, t), 1)
        bias = slope * krel.astype(F32)
        sa, sb = carry
        return update(qa, kt, v, bias, sa), update(qb, kt, v, bias, sb)

    def init():
        return (jnp.full((t, 1), NEG_BIG, F32), jnp.zeros((t, 1), F32), jnp.zeros((t, V_DIM), F32))

    carry = lax.fori_loop(0, qi, off_diag, (init(), init()))

    qrel = lax.broadcasted_iota(jnp.int32, (t, t), 0)
    krel = lax.broadcasted_iota(jnp.int32, (t, t), 1)
    allowed = (krel // CHUNK) <= (qrel // CHUNK)
    shifted = slope * (qrel - jnp.abs(qrel - krel)).astype(F32)
    bias_d = jnp.where(allowed, shifted, NEG_BIG)
    kt = kt_scr[qi]
    v = v_ref[0, pl.ds(pl.multiple_of(qi * t, t), t), :]
    (_, l1, acc1) = update(qa, kt, v, bias_d, carry[0])
    (_, l2, acc2) = update(qb, kt, v, bias_d, carry[1])

    lam = (jnp.exp(jnp.sum(lq1_ref[...] * lk1_ref[...], axis=-1, keepdims=True))
           - jnp.exp(jnp.sum(lq2_ref[...] * lk2_ref[...], axis=-1, keepdims=True)) + lam_init)
    o = acc1 / l1 - lam * (acc2 / l2)
    o = _rms(o, SUBLN_EPS) * gsub_ref[...] * (1.0 - lam_init)
    o_ref[...] = o.astype(BF16)


def _attn(slopes, lq1, lk1, lq2, lk2, proj, gsub, lam_init, t=512):
    s = proj.shape[1]
    nt = s // t
    smem = pl.BlockSpec(memory_space=pltpu.SMEM)
    vec = lambda n: pl.BlockSpec((1, n), lambda h, i: (0, 0))
    return pl.pallas_call(
        functools.partial(_attn_kernel, t=t, lam_init=lam_init),
        out_shape=jax.ShapeDtypeStruct((s, N_HEADS * V_DIM), BF16),
        grid=(N_HEADS, nt),
        in_specs=[smem, vec(QK_DIM), vec(QK_DIM), vec(QK_DIM), vec(QK_DIM),
                  pl.BlockSpec((1, t, LANES), lambda h, i: (h, i, 0)),
                  pl.BlockSpec((1, s, LANES), lambda h, i: (N_HEADS + h, 0, 0)),
                  pl.BlockSpec((1, s, LANES), lambda h, i: (2 * N_HEADS + h, 0, 0)),
                  vec(V_DIM)],
        out_specs=pl.BlockSpec((t, V_DIM), lambda h, i: (i, h)),
        scratch_shapes=[pltpu.VMEM((nt, LANES, t), BF16)],
        compiler_params=pltpu.CompilerParams(dimension_semantics=("arbitrary", "arbitrary"),
                                             vmem_limit_bytes=VMEM_LIMIT),
        name="attn",
    )(slopes, lq1, lk1, lq2, lk2, proj, proj, proj, gsub)


def _route(logits):
    lane = lax.broadcasted_iota(jnp.int32, logits.shape, 1).astype(F32)
    far = float(4 * LANES)
    gl = jnp.where(lane < N_GROUPS, logits, NEG_BIG)
    gmax = jnp.max(gl, axis=-1, keepdims=True)
    g_w = 1.0 / jnp.sum(jnp.exp(gl - gmax), axis=-1, keepdims=True)
    g_idx = jnp.min(jnp.where(gl == gmax, lane, far), axis=-1, keepdims=True)
    lo = N_GROUPS + EXPERTS_PER_GROUP * g_idx
    el = jnp.where((lane >= lo) & (lane < lo + EXPERTS_PER_GROUP), logits, NEG_BIG)
    e1 = jnp.max(el, axis=-1, keepdims=True)
    i1 = jnp.min(jnp.where(el == e1, lane, far), axis=-1, keepdims=True)
    el2 = jnp.where(lane == i1, NEG_BIG, el)
    e2 = jnp.max(el2, axis=-1, keepdims=True)
    i2 = jnp.min(jnp.where(el2 == e2, lane, far), axis=-1, keepdims=True)
    r = jnp.exp(e2 - e1)
    w1 = 1.0 / (1.0 + r)
    w2 = r * w1
    out = jnp.where(lane == 0.0, i1 - N_GROUPS, 0.0)
    out = jnp.where(lane == 1.0, i2 - N_GROUPS, out)
    out = jnp.where(lane == 2.0, g_w * w1, out)
    out = jnp.where(lane == 3.0, g_w * w2, out)
    return out


def _outproj_kernel(ya_ref, b_ref, c_ref, u_ref, x_ref, wconv_ref, wout_ref, gpost_ref, gt_ref,
                    gpre_ref, sc_ref, sh_ref, wr_ref, br_ref,
                    x1_ref, h2_ref, rinfo_ref, zbuf, ybuf, *, tm, nc):
    i = pl.program_id(0)
    halo = 8

    @pl.when(i == 0)
    def _():
        zbuf[0:halo, :] = jnp.zeros((halo, zbuf.shape[1]), F32)

    @pl.when(i > 0)
    def _():
        zbuf[0:halo, :] = zbuf[tm:tm + halo, :]

    for c in range(nc):
        cols = slice(c * LANES, (c + 1) * LANES)
        zbuf[halo:halo + tm, cols] = c_ref[c].astype(F32) * u_ref[c].astype(F32)
    d_attn = ya_ref.shape[1]
    ybuf[:, 0:d_attn] = ya_ref[...]
    for c in range(nc):
        cols = slice(c * LANES, (c + 1) * LANES)
        conv = (wconv_ref[0:1, cols] * zbuf[halo - 2:halo - 2 + tm, cols]
                + wconv_ref[1:2, cols] * zbuf[halo - 1:halo - 1 + tm, cols]
                + wconv_ref[2:3, cols] * zbuf[halo:halo + tm, cols])
        ybuf[:, d_attn + c * LANES:d_attn + (c + 1) * LANES] = (b_ref[c].astype(F32) * conv).astype(BF16)

    y = jnp.dot(ybuf[...], wout_ref[...], preferred_element_type=F32)
    x1 = x_ref[...] + gt_ref[...] * (_rms(y, NORM_EPS) * gpost_ref[...])
    x1_ref[...] = x1
    h2 = (_rms(x1, NORM_EPS) * gpre_ref[...]) * (1.0 + sc_ref[...]) + sh_ref[...]
    h2b = h2.astype(BF16)
    h2_ref[...] = h2b
    logits = jnp.dot(h2b, wr_ref[...], preferred_element_type=F32) + br_ref[...]
    rinfo_ref[...] = _route(logits)


def _outproj(y_attn, proj, x2, w_conv, wout_bf, gpost, gt, gpre, sc, sh, wr_bf, br, tm=512):
    s, d = x2.shape
    d_attn = y_attn.shape[1]
    nc = (d - d_attn) // LANES
    base = 3 * d_attn // LANES
    row = lambda n: pl.BlockSpec((1, n), lambda i: (0, 0))
    blk = lambda g: pl.BlockSpec((nc, tm, LANES), lambda i, g=g: (base // nc + g, i, 0))
    return pl.pallas_call(
        functools.partial(_outproj_kernel, tm=tm, nc=nc),
        out_shape=(jax.ShapeDtypeStruct((s, d), F32),
                   jax.ShapeDtypeStruct((s, d), BF16),
                   jax.ShapeDtypeStruct((s, LANES), F32)),
        grid=(s // tm,),
        in_specs=[pl.BlockSpec((tm, d_attn), lambda i: (i, 0)),
                  blk(0), blk(1), blk(2),
                  pl.BlockSpec((tm, d), lambda i: (i, 0)),
                  pl.BlockSpec((CONV_WIDTH, d - d_attn), lambda i: (0, 0)),
                  pl.BlockSpec((d, d), lambda i: (0, 0)),
                  row(d), row(d), row(d), row(d), row(d),
                  pl.BlockSpec((d, LANES), lambda i: (0, 0)),
                  row(LANES)],
        out_specs=(pl.BlockSpec((tm, d), lambda i: (i, 0)),
                   pl.BlockSpec((tm, d), lambda i: (i, 0)),
                   pl.BlockSpec((tm, LANES), lambda i: (i, 0))),
        scratch_shapes=[pltpu.VMEM((tm + 8, d - d_attn), F32), pltpu.VMEM((tm, d), BF16)],
        compiler_params=pltpu.CompilerParams(dimension_semantics=("arbitrary",),
                                             vmem_limit_bytes=VMEM_LIMIT),
        name="outproj",
    )(y_attn, proj, proj, proj, x2, w_conv, wout_bf, gpost, gt, gpre, sc, sh, wr_bf, br)


def _moe_kernel(h2_ref, rinfo_ref, wg_ref, wu_ref, wd_ref, x1_ref, gpost_ref, gt_ref, o_ref, acc):
    e = pl.program_id(1)

    @pl.when(e == 0)
    def _():
        acc[...] = jnp.zeros_like(acc)

    r = rinfo_ref[...]
    ef = e.astype(F32)
    gate = (jnp.where(r[:, 0:1] == ef, r[:, 2:3], 0.0)
            + jnp.where(r[:, 1:2] == ef, r[:, 3:4], 0.0))
    x = h2_ref[...]
    hg = jnp.dot(x, wg_ref[0], preferred_element_type=F32)
    hu = jnp.dot(x, wu_ref[0], preferred_element_type=F32)
    a = (hg * _sigmoid(hg)) * hu * gate
    acc[...] += jnp.dot(a.astype(BF16), wd_ref[0], preferred_element_type=F32)

    @pl.when(e == pl.num_programs(1) - 1)
    def _():
        o_ref[...] = x1_ref[...] + gt_ref[...] * (_rms(acc[...], NORM_EPS) * gpost_ref[...])


def _moe(h2, rinfo, wg_bf, wu_bf, wd_bf, x1, gpost, gt, tm=512):
    s, d = h2.shape
    ne, _, f = wg_bf.shape
    row = pl.BlockSpec((1, d), lambda i, e: (0, 0))
    return pl.pallas_call(
        _moe_kernel,
        out_shape=jax.ShapeDtypeStruct((s, d), F32),
        grid=(s // tm, ne),
        in_specs=[pl.BlockSpec((tm, d), lambda i, e: (i, 0)),
                  pl.BlockSpec((tm, LANES), lambda i, e: (i, 0)),
                  pl.BlockSpec((1, d, f), lambda i, e: (e, 0, 0)),
                  pl.BlockSpec((1, d, f), lambda i, e: (e, 0, 0)),
                  pl.BlockSpec((1, f, d), lambda i, e: (e, 0, 0)),
                  pl.BlockSpec((tm, d), lambda i, e: (i, 0)),
                  row, row],
        out_specs=pl.BlockSpec((tm, d), lambda i, e: (i, 0)),
        scratch_shapes=[pltpu.VMEM((tm, d), F32)],
        compiler_params=pltpu.CompilerParams(dimension_semantics=("arbitrary", "arbitrary"),
                                             vmem_limit_bytes=VMEM_LIMIT),
        name="moe",
    )(h2, rinfo, wg_bf, wu_bf, wd_bf, x1, gpost, gt)


def kernel(x, c, w_ada, b_ada, g_pre1, w_in, lam_q1, lam_k1, lam_q2, lam_k2, g_subln, w_conv, w_out,
           g_post1, g_pre2, w_router_g, b_router_g, w_router_e, b_router_e, w_gate, w_up, w_down, g_post2):
    b, s, d = x.shape
    assert b == 1
    depth = w_ada.shape[0]
    slopes = jnp.asarray([2.0 ** (-8.0 * (i + 1) / N_HEADS) for i in range(N_HEADS)], F32)
    x2 = x.reshape(s, d)
    for l in range(depth):
        lam_init = 0.8 - 0.6 * math.exp(-0.3 * l)
        mod = _ada(c.reshape(d, 1), w_ada[l], b_ada[l].reshape(1, -1))
        sh1, sc1, gt1, sh2, sc2, gt2 = [mod[:, k * d:(k + 1) * d] for k in range(6)]
        proj = _inproj(x2, g_pre1[l].reshape(1, d), sc1, sh1, w_in[l].astype(BF16))
        y_attn = _attn(slopes, lam_q1[l].reshape(1, -1), lam_k1[l].reshape(1, -1),
                       lam_q2[l].reshape(1, -1), lam_k2[l].reshape(1, -1), proj,
                       g_subln[l].reshape(1, -1), lam_init)
        wr = jnp.concatenate([w_router_g[l], w_router_e[l].reshape(d, N_EXPERTS),
                              jnp.zeros((d, LANES - N_GROUPS - N_EXPERTS), F32)], axis=1).astype(BF16)
        br = jnp.concatenate([b_router_g[l], b_router_e[l].reshape(N_EXPERTS),
                              jnp.zeros((LANES - N_GROUPS - N_EXPERTS,), F32)]).reshape(1, LANES)
        x1, h2, rinfo = _outproj(y_attn, proj, x2, w_conv[l], w_out[l].astype(BF16),
                                 g_post1[l].reshape(1, d), gt1, g_pre2[l].reshape(1, d), sc2, sh2, wr, br)
        x2 = _moe(h2, rinfo, w_gate[l].astype(BF16), w_up[l].astype(BF16), w_down[l].astype(BF16),
                  x1, g_post2[l].reshape(1, d), gt2)
    return x2.reshape(b, s, d)
```

```python
import functools
import math

import jax
import jax.numpy as jnp
from jax import lax
from jax.experimental import pallas as pl
from jax.experimental.pallas import tpu as pltpu

F32 = jnp.float32
BF16 = jnp.bfloat16

LANES = 128
CHUNK = 64
N_HEADS = 8
QK_DIM = 64
V_DIM = 128
CONV_WIDTH = 3
N_GROUPS = 4
EXPERTS_PER_GROUP = 8
N_EXPERTS = N_GROUPS * EXPERTS_PER_GROUP
NORM_EPS = 1e-6
SUBLN_EPS = 1e-5
NEG_BIG = -1e30
VMEM_LIMIT = 56 * 1024 * 1024


def _rms(x, eps):
    return x * lax.rsqrt(jnp.mean(x * x, axis=-1, keepdims=True) + eps)


def _sigmoid(x):
    return 1.0 / (1.0 + jnp.exp(-x))


def _ada_kernel(c_ref, w_ref, b_ref, o_ref):
    c = c_ref[...]
    s = c * _sigmoid(c)
    o_ref[...] = jnp.sum(w_ref[...] * s, axis=0, keepdims=True) + b_ref[...]


def _ada(c_col, w_ada, b_ada, tn=1024):
    d, n = w_ada.shape
    return pl.pallas_call(
        _ada_kernel,
        out_shape=jax.ShapeDtypeStruct((1, n), F32),
        grid=(n // tn,),
        in_specs=[pl.BlockSpec((d, 1), lambda j: (0, 0)),
                  pl.BlockSpec((d, tn), lambda j: (0, j)),
                  pl.BlockSpec((1, tn), lambda j: (0, j))],
        out_specs=pl.BlockSpec((1, tn), lambda j: (0, j)),
        compiler_params=pltpu.CompilerParams(dimension_semantics=("arbitrary",),
                                             vmem_limit_bytes=VMEM_LIMIT),
        name="ada",
    )(c_col, w_ada, b_ada)


def _inproj_kernel(x_ref, g_ref, sc_ref, sh_ref, w_ref, o_ref, h_scr, *, cn):
    @pl.when(pl.program_id(1) == 0)
    def _():
        y = _rms(x_ref[...], NORM_EPS) * g_ref[...]
        h_scr[...] = (y * (1.0 + sc_ref[...]) + sh_ref[...]).astype(BF16)

    acc = jnp.dot(h_scr[...], w_ref[...], preferred_element_type=F32)
    for c in range(cn):
        o_ref[c] = acc[:, c * LANES:(c + 1) * LANES].astype(BF16)


def _inproj(x2, g, sc, sh, w_bf, tm=1024, cn=4):
    s, d = x2.shape
    n = w_bf.shape[1]
    nchunks = n // LANES
    row = lambda i, j: (0, 0)
    return pl.pallas_call(
        functools.partial(_inproj_kernel, cn=cn),
        out_shape=jax.ShapeDtypeStruct((nchunks, s, LANES), BF16),
        grid=(s // tm, nchunks // cn),
        in_specs=[pl.BlockSpec((tm, d), lambda i, j: (i, 0)),
                  pl.BlockSpec((1, d), row),
                  pl.BlockSpec((1, d), row),
                  pl.BlockSpec((1, d), row),
                  pl.BlockSpec((d, cn * LANES), lambda i, j: (0, j))],
        out_specs=pl.BlockSpec((cn, tm, LANES), lambda i, j: (j, i, 0)),
        scratch_shapes=[pltpu.VMEM((tm, d), BF16)],
        compiler_params=pltpu.CompilerParams(dimension_semantics=("arbitrary", "arbitrary"),
                                             vmem_limit_bytes=VMEM_LIMIT),
        name="inproj",
    )(x2, g, sc, sh, w_bf)


def _attn_kernel(slope_ref, lq1_ref, lk1_ref, lq2_ref, lk2_ref, q_ref, k_ref, v_ref, gsub_ref,
                 o_ref, kt_scr, *, t, lam_init):
    h = pl.program_id(0)
    qi = pl.program_id(1)
    nkt = kt_scr.shape[0]

    @pl.when(qi == 0)
    def _():
        def tr(c, carry):
            kc = k_ref[0, pl.ds(pl.multiple_of(c * t, t), t), :]
            kt_scr[c] = kc.astype(F32).T.astype(BF16)
            return carry
        lax.fori_loop(0, nkt, tr, 0)

    slope = slope_ref[h]
    q = q_ref[0].astype(F32) * (QK_DIM ** -0.5)
    lane = lax.broadcasted_iota(jnp.int32, q.shape, 1)
    qa = jnp.where(lane < QK_DIM, q, 0.0).astype(BF16)
    qb = jnp.where(lane >= QK_DIM, q, 0.0).astype(BF16)

    def update(qm, kt, v, bias, st):
        m, l, acc = st
        s = jnp.dot(qm, kt, preferred_element_type=F32) + bias
        m_new = jnp.maximum(m, jnp.max(s, axis=-1, keepdims=True))
        alpha = jnp.exp(m - m_new)
        p = jnp.exp(s - m_new)
        l = alpha * l + jnp.sum(p, axis=-1, keepdims=True)
        acc = alpha * acc + jnp.dot(p.astype(BF16), v, preferred_element_type=F32)
        return m_new, l, acc

    def off_diag(ki, carry):
        kt = kt_scr[ki]
        v = v_ref[0, pl.ds(pl.multiple_of(ki * t, t), t), :]
        krel = (ki - qi) * t + lax.broadcasted_iota(jnp.int32, (1, t), 1)
        bias = slope * krel.astype(F32)
        sa, sb = carry
        return update(qa, kt, v, bias, sa), update(qb, kt, v, bias, sb)

    def init():
        return (jnp.full((t, 1), NEG_BIG, F32), jnp.zeros((t, 1), F32), jnp.zeros((t, V_DIM), F32))

    carry = lax.fori_loop(0, qi, off_diag, (init(), init()))

    qrel = lax.broadcasted_iota(jnp.int32, (t, t), 0)
    krel = lax.broadcasted_iota(jnp.int32, (t, t), 1)
    allowed = (krel // CHUNK) <= (qrel // CHUNK)
    shifted = slope * (qrel - jnp.abs(qrel - krel)).astype(F32)
    bias_d = jnp.where(allowed, shifted, NEG_BIG)
    kt = kt_scr[qi]
    v = v_ref[0, pl.ds(pl.multiple_of(qi * t, t), t), :]
    (_, l1, acc1) = update(qa, kt, v, bias_d, carry[0])
    (_, l2, acc2) = update(qb, kt, v, bias_d, carry[1])

    lam = (jnp.exp(jnp.sum(lq1_ref[...] * lk1_ref[...], axis=-1, keepdims=True))
           - jnp.exp(jnp.sum(lq2_ref[...] * lk2_ref[...], axis=-1, keepdims=True)) + lam_init)
    o = acc1 / l1 - lam * (acc2 / l2)
    o = _rms(o, SUBLN_EPS) * gsub_ref[...] * (1.0 - lam_init)
    o_ref[...] = o.astype(BF16)


def _attn(slopes, lq1, lk1, lq2, lk2, proj, gsub, lam_init, t=512):
    s = proj.shape[1]
    nt = s // t
    smem = pl.BlockSpec(memory_space=pltpu.SMEM)
    vec = lambda n: pl.BlockSpec((1, n), lambda h, i: (0, 0))
    return pl.pallas_call(
        functools.partial(_attn_kernel, t=t, lam_init=lam_init),
        out_shape=jax.ShapeDtypeStruct((s, N_HEADS * V_DIM), BF16),
        grid=(N_HEADS, nt),
        in_specs=[smem, vec(QK_DIM), vec(QK_DIM), vec(QK_DIM), vec(QK_DIM),
                  pl.BlockSpec((1, t, LANES), lambda h, i: (h, i, 0)),
                  pl.BlockSpec((1, s, LANES), lambda h, i: (N_HEADS + h, 0, 0)),
                  pl.BlockSpec((1, s, LANES), lambda h, i: (2 * N_HEADS + h, 0, 0)),
                  vec(V_DIM)],
        out_specs=pl.BlockSpec((t, V_DIM), lambda h, i: (i, h)),
        scratch_shapes=[pltpu.VMEM((nt, LANES, t), BF16)],
        compiler_params=pltpu.CompilerParams(dimension_semantics=("arbitrary", "arbitrary"),
                                             vmem_limit_bytes=VMEM_LIMIT),
        name="attn",
    )(slopes, lq1, lk1, lq2, lk2, proj, proj, proj, gsub)


def _route(logits):
    lane = lax.broadcasted_iota(jnp.int32, logits.shape, 1).astype(F32)
    far = float(4 * LANES)
    gl = jnp.where(lane < N_GROUPS, logits, NEG_BIG)
    gmax = jnp.max(gl, axis=-1, keepdims=True)
    g_w = 1.0 / jnp.sum(jnp.exp(gl - gmax), axis=-1, keepdims=True)
    g_idx = jnp.min(jnp.where(gl == gmax, lane, far), axis=-1, keepdims=True)
    lo = N_GROUPS + EXPERTS_PER_GROUP * g_idx
    el = jnp.where((lane >= lo) & (lane < lo + EXPERTS_PER_GROUP), logits, NEG_BIG)
    e1 = jnp.max(el, axis=-1, keepdims=True)
    i1 = jnp.min(jnp.where(el == e1, lane, far), axis=-1, keepdims=True)
    el2 = jnp.where(lane == i1, NEG_BIG, el)
    e2 = jnp.max(el2, axis=-1, keepdims=True)
    i2 = jnp.min(jnp.where(el2 == e2, lane, far), axis=-1, keepdims=True)
    r = jnp.exp(e2 - e1)
    w1 = 1.0 / (1.0 + r)
    w2 = r * w1
    out = jnp.where(lane == 0.0, i1 - N_GROUPS, 0.0)
    out = jnp.where(lane == 1.0, i2 - N_GROUPS, out)
    out = jnp.where(lane == 2.0, g_w * w1, out)
    out = jnp.where(lane == 3.0, g_w * w2, out)
    return out


def _outproj_kernel(ya_ref, b_ref, c_ref, u_ref, x_ref, wconv_ref, wout_ref, gpost_ref, gt_ref,
                    gpre_ref, sc_ref, sh_ref, wr_ref, br_ref,
                    x1_ref, h2_ref, rinfo_ref, zbuf, ybuf, *, tm, nc):
    i = pl.program_id(0)
    halo = 8

    @pl.when(i == 0)
    def _():
        zbuf[0:halo, :] = jnp.zeros((halo, zbuf.shape[1]), F32)

    @pl.when(i > 0)
    def _():
        zbuf[0:halo, :] = zbuf[tm:tm + halo, :]

    for c in range(nc):
        cols = slice(c * LANES, (c + 1) * LANES)
        zbuf[halo:halo + tm, cols] = c_ref[c].astype(F32) * u_ref[c].astype(F32)
    d_attn = ya_ref.shape[1]
    ybuf[:, 0:d_attn] = ya_ref[...]
    for c in range(nc):
        cols = slice(c * LANES, (c + 1) * LANES)
        conv = (wconv_ref[0:1, cols] * zbuf[halo - 2:halo - 2 + tm, cols]
                + wconv_ref[1:2, cols] * zbuf[halo - 1:halo - 1 + tm, cols]
                + wconv_ref[2:3, cols] * zbuf[halo:halo + tm, cols])
        ybuf[:, d_attn + c * LANES:d_attn + (c + 1) * LANES] = (b_ref[c].astype(F32) * conv).astype(BF16)

    y = jnp.dot(ybuf[...], wout_ref[...], preferred_element_type=F32)
    x1 = x_ref[...] + gt_ref[...] * (_rms(y, NORM_EPS) * gpost_ref[...])
    x1_ref[...] = x1
    h2 = (_rms(x1, NORM_EPS) * gpre_ref[...]) * (1.0 + sc_ref[...]) + sh_ref[...]
    h2_ref[...] = h2
    logits = jnp.dot(h2.astype(BF16), wr_ref[...], preferred_element_type=F32) + br_ref[...]
    rinfo_ref[...] = _route(logits)


def _outproj(y_attn, proj, x2, w_conv, wout_bf, gpost, gt, gpre, sc, sh, wr_bf, br, tm=512):
    s, d = x2.shape
    d_attn = y_attn.shape[1]
    nc = (d - d_attn) // LANES
    base = 3 * d_attn // LANES
    row = lambda n: pl.BlockSpec((1, n), lambda i: (0, 0))
    blk = lambda g: pl.BlockSpec((nc, tm, LANES), lambda i, g=g: (base // nc + g, i, 0))
    return pl.pallas_call(
        functools.partial(_outproj_kernel, tm=tm, nc=nc),
        out_shape=(jax.ShapeDtypeStruct((s, d), F32),
                   jax.ShapeDtypeStruct((s, d), F32),
                   jax.ShapeDtypeStruct((s, LANES), F32)),
        grid=(s // tm,),
        in_specs=[pl.BlockSpec((tm, d_attn), lambda i: (i, 0)),
                  blk(0), blk(1), blk(2),
                  pl.BlockSpec((tm, d), lambda i: (i, 0)),
                  pl.BlockSpec((CONV_WIDTH, d - d_attn), lambda i: (0, 0)),
                  pl.BlockSpec((d, d), lambda i: (0, 0)),
                  row(d), row(d), row(d), row(d), row(d),
                  pl.BlockSpec((d, LANES), lambda i: (0, 0)),
                  row(LANES)],
        out_specs=(pl.BlockSpec((tm, d), lambda i: (i, 0)),
                   pl.BlockSpec((tm, d), lambda i: (i, 0)),
                   pl.BlockSpec((tm, LANES), lambda i: (i, 0))),
        scratch_shapes=[pltpu.VMEM((tm + 8, d - d_attn), F32), pltpu.VMEM((tm, d), BF16)],
        compiler_params=pltpu.CompilerParams(dimension_semantics=("arbitrary",),
                                             vmem_limit_bytes=VMEM_LIMIT),
        name="outproj",
    )(y_attn, proj, proj, proj, x2, w_conv, wout_bf, gpost, gt, gpre, sc, sh, wr_bf, br)


MOE_TILE = 256


def _moe_num_tiles(s):
    return (2 * s) // MOE_TILE + N_EXPERTS


def _plan_kernel(rinfo_ref, pos_ref, tinfo_ref, cnt_scr, base_scr, *, tm, ntm):
    ph = pl.program_id(0)
    i = pl.program_id(1)
    r = rinfo_ref[...]
    lane = lax.broadcasted_iota(jnp.int32, r.shape, 1).astype(F32)
    oh0 = jnp.where(lane == r[:, 0:1], 1.0, 0.0)
    oh1 = jnp.where(lane == r[:, 1:2], 1.0, 0.0)
    both = oh0 + oh1
    tile_cnt = jnp.sum(both, axis=0, keepdims=True)

    @pl.when((ph == 0) & (i == 0))
    def _():
        cnt_scr[...] = jnp.zeros_like(cnt_scr)

    @pl.when(ph == 0)
    def _():
        cnt_scr[...] += tile_cnt

    @pl.when((ph == 1) & (i == 0))
    def _():
        ktiles = jnp.floor((cnt_scr[...] + (MOE_TILE - 1)) * (1.0 / MOE_TILE))
        rr = lax.broadcasted_iota(jnp.int32, (LANES, LANES), 0)
        cc = lax.broadcasted_iota(jnp.int32, (LANES, LANES), 1)
        upper = jnp.where(rr < cc, 1.0, 0.0).astype(BF16)
        first = jnp.dot(jnp.broadcast_to(ktiles, (8, LANES)).astype(BF16), upper,
                        preferred_element_type=F32)[0:1, :]
        base_scr[...] = first * MOE_TILE
        ti = lax.broadcasted_iota(jnp.int32, (ntm, LANES), 0).astype(F32)
        el = lax.broadcasted_iota(jnp.int32, (ntm, LANES), 1).astype(F32)
        owned = jnp.where(ti >= first, jnp.where(ti < first + ktiles, 1.0, 0.0), 0.0)
        texp = jnp.sum(owned * el, axis=1, keepdims=True)
        nact = jnp.sum(ktiles, axis=1, keepdims=True)
        tinfo_ref[...] = jnp.where(el == 0.0, texp, jnp.where(el == 1.0, nact, 0.0))

    @pl.when(ph == 1)
    def _():
        rr = lax.broadcasted_iota(jnp.int32, (tm, tm), 0)
        cc = lax.broadcasted_iota(jnp.int32, (tm, tm), 1)
        lower = jnp.where(cc < rr, 1.0, 0.0).astype(BF16)
        before = jnp.dot(lower, both.astype(BF16), preferred_element_type=F32)
        val = before + base_scr[...]
        p0 = jnp.sum(oh0 * val, axis=1, keepdims=True)
        p1 = jnp.sum(oh1 * val, axis=1, keepdims=True)
        pos_ref[...] = jnp.where(lane == 0.0, p0, jnp.where(lane == 1.0, p1, 0.0))
        base_scr[...] += tile_cnt


def _plan(rinfo, tm=256):
    s = rinfo.shape[0]
    ntm = _moe_num_tiles(s)
    return pl.pallas_call(
        functools.partial(_plan_kernel, tm=tm, ntm=ntm),
        out_shape=(jax.ShapeDtypeStruct((s, LANES), F32), jax.ShapeDtypeStruct((ntm, LANES), F32)),
        grid=(2, s // tm),
        in_specs=[pl.BlockSpec((tm, LANES), lambda p, i: (i, 0))],
        out_specs=(pl.BlockSpec((tm, LANES), lambda p, i: (i * p, 0)),
                   pl.BlockSpec((ntm, LANES), lambda p, i: (0, 0))),
        scratch_shapes=[pltpu.VMEM((1, LANES), F32), pltpu.VMEM((1, LANES), F32)],
        compiler_params=pltpu.CompilerParams(dimension_semantics=("arbitrary", "arbitrary"),
                                             vmem_limit_bytes=VMEM_LIMIT),
        name="plan",
    )(rinfo)


def _dispatch_kernel(pos_ref, texp_ref, nact_ref, h2_ref, xs_ref, zero_scr, sem, zsem, *, tm):
    i = pl.program_id(0)

    @pl.when(i == 0)
    def _():
        zero_scr[...] = jnp.zeros_like(zero_scr)
        nact = nact_ref[0]
        ntm = xs_ref.shape[0] // MOE_TILE

        def is_last(t):
            nxt = texp_ref[jnp.minimum(t + 1, nact - 1)]
            return (t >= nact - 1) | (nxt != texp_ref[jnp.minimum(t, nact - 1)])

        def zstart(t, carry):
            @pl.when(is_last(t))
            def _():
                pltpu.make_async_copy(zero_scr, xs_ref.at[pl.ds(pl.multiple_of(t * MOE_TILE, MOE_TILE), MOE_TILE), :],
                                      zsem).start()
            return carry

        def zwait(t, carry):
            @pl.when(is_last(t))
            def _():
                pltpu.make_async_copy(zero_scr, xs_ref.at[pl.ds(0, MOE_TILE), :], zsem).wait()
            return carry

        lax.fori_loop(0, ntm, zstart, 0)
        lax.fori_loop(0, ntm, zwait, 0)

    def start(r, carry):
        tok = i * tm + r
        for slot in range(2):
            p = pos_ref[2 * tok + slot]
            pltpu.make_async_copy(h2_ref.at[pl.ds(r, 1), :], xs_ref.at[pl.ds(p, 1), :], sem).start()
        return carry

    def wait(r, carry):
        for slot in range(2):
            pltpu.make_async_copy(h2_ref.at[pl.ds(0, 1), :], xs_ref.at[pl.ds(0, 1), :], sem).wait()
        return carry

    lax.fori_loop(0, tm, start, 0)
    lax.fori_loop(0, tm, wait, 0)


def _dispatch(pos_flat, texp, nact, h2, tm=256):
    s, d = h2.shape
    ntm = _moe_num_tiles(s)
    return pl.pallas_call(
        functools.partial(_dispatch_kernel, tm=tm),
        out_shape=jax.ShapeDtypeStruct((ntm * MOE_TILE, d), F32),
        grid_spec=pltpu.PrefetchScalarGridSpec(
            num_scalar_prefetch=3,
            grid=(s // tm,),
            in_specs=[pl.BlockSpec((tm, d), lambda i, *_: (i, 0))],
            out_specs=pl.BlockSpec(memory_space=pl.ANY),
            scratch_shapes=[pltpu.VMEM((MOE_TILE, d), F32), pltpu.SemaphoreType.DMA, pltpu.SemaphoreType.DMA]),
        compiler_params=pltpu.CompilerParams(dimension_semantics=("arbitrary",),
                                             vmem_limit_bytes=VMEM_LIMIT),
        name="dispatch",
    )(pos_flat, texp, nact, h2)


def _experts_kernel(texp_ref, nact_ref, xs_ref, wg_ref, wu_ref, wd_ref, ys_ref, wg_bf, wu_bf, wd_bf):
    i = pl.program_id(0)
    nact = nact_ref[0]
    j = jnp.minimum(i, nact - 1)
    fresh = (i == 0) | (texp_ref[j] != texp_ref[jnp.maximum(j - 1, 0)])

    @pl.when((i < nact) & fresh)
    def _():
        wg_bf[...] = wg_ref[0].astype(BF16)
        wu_bf[...] = wu_ref[0].astype(BF16)
        wd_bf[...] = wd_ref[0].astype(BF16)

    @pl.when(i < nact)
    def _():
        x = xs_ref[...].astype(BF16)
        hg = jnp.dot(x, wg_bf[...], preferred_element_type=F32)
        hu = jnp.dot(x, wu_bf[...], preferred_element_type=F32)
        a = (hg * _sigmoid(hg)) * hu
        ys_ref[...] = jnp.dot(a.astype(BF16), wd_bf[...], preferred_element_type=F32)

    @pl.when(i >= nact)
    def _():
        ys_ref[...] = jnp.zeros_like(ys_ref)


def _experts(texp, nact, xs, w_gate, w_up, w_down):
    p, d = xs.shape
    ntm = p // MOE_TILE
    _, _, f = w_gate.shape

    def tile(i, texp_ref, nact_ref):
        return (jnp.minimum(i, nact_ref[0] - 1), 0)

    def wsel(i, texp_ref, nact_ref):
        return (texp_ref[jnp.minimum(i, nact_ref[0] - 1)], 0, 0)

    return pl.pallas_call(
        _experts_kernel,
        out_shape=jax.ShapeDtypeStruct((p, d), F32),
        grid_spec=pltpu.PrefetchScalarGridSpec(
            num_scalar_prefetch=2,
            grid=(ntm,),
            in_specs=[pl.BlockSpec((MOE_TILE, d), tile),
                      pl.BlockSpec((1, d, f), wsel),
                      pl.BlockSpec((1, d, f), wsel),
                      pl.BlockSpec((1, f, d), wsel)],
            out_specs=pl.BlockSpec((MOE_TILE, d), lambda i, *_: (i, 0)),
            scratch_shapes=[pltpu.VMEM((d, f), BF16), pltpu.VMEM((d, f), BF16), pltpu.VMEM((f, d), BF16)]),
        compiler_params=pltpu.CompilerParams(dimension_semantics=("arbitrary",),
                                             vmem_limit_bytes=VMEM_LIMIT),
        name="experts",
    )(texp, nact, xs, w_gate, w_up, w_down)


def _combine_kernel(pos_ref, ys_ref, rinfo_ref, x1_ref, gpost_ref, gt_ref, o_ref, ybuf, sem, *, tm):
    i = pl.program_id(0)

    def start(r, carry):
        tok = i * tm + r
        for slot in range(2):
            p = pos_ref[2 * tok + slot]
            pltpu.make_async_copy(ys_ref.at[pl.ds(p, 1), :], ybuf.at[slot, pl.ds(r, 1), :], sem).start()
        return carry

    def wait(r, carry):
        for slot in range(2):
            pltpu.make_async_copy(ys_ref.at[pl.ds(0, 1), :], ybuf.at[slot, pl.ds(0, 1), :], sem).wait()
        return carry

    lax.fori_loop(0, tm, start, 0)
    lax.fori_loop(0, tm, wait, 0)
    r = rinfo_ref[...]
    y = r[:, 2:3] * ybuf[0] + r[:, 3:4] * ybuf[1]
    o_ref[...] = x1_ref[...] + gt_ref[...] * (_rms(y, NORM_EPS) * gpost_ref[...])


def _combine(pos_flat, ys, rinfo, x1, gpost, gt, tm=256):
    s, d = x1.shape
    row = pl.BlockSpec((1, d), lambda i, *_: (0, 0))
    return pl.pallas_call(
        functools.partial(_combine_kernel, tm=tm),
        out_shape=jax.ShapeDtypeStruct((s, d), F32),
        grid_spec=pltpu.PrefetchScalarGridSpec(
            num_scalar_prefetch=1,
            grid=(s // tm,),
            in_specs=[pl.BlockSpec(memory_space=pl.ANY),
                      pl.BlockSpec((tm, LANES), lambda i, *_: (i, 0)),
                      pl.BlockSpec((tm, d), lambda i, *_: (i, 0)),
                      row, row],
            out_specs=pl.BlockSpec((tm, d), lambda i, *_: (i, 0)),
            scratch_shapes=[pltpu.VMEM((2, tm, d), F32), pltpu.SemaphoreType.DMA]),
        compiler_params=pltpu.CompilerParams(dimension_semantics=("arbitrary",),
                                             vmem_limit_bytes=VMEM_LIMIT),
        name="combine",
    )(pos_flat, ys, rinfo, x1, gpost, gt)


def _moe(h2, rinfo, w_gate, w_up, w_down, x1, gpost, gt):
    pos, tinfo = _plan(rinfo)
    pos_flat = pos[:, 0:2].astype(jnp.int32).reshape(-1)
    texp = tinfo[:, 0].astype(jnp.int32)
    nact = tinfo[0:1, 1].astype(jnp.int32)
    xs = _dispatch(pos_flat, texp, nact, h2)
    ys = _experts(texp, nact, xs, w_gate, w_up, w_down)
    return _combine(pos_flat, ys, rinfo, x1, gpost, gt)


def kernel(x, c, w_ada, b_ada, g_pre1, w_in, lam_q1, lam_k1, lam_q2, lam_k2, g_subln, w_conv, w_out,
           g_post1, g_pre2, w_router_g, b_router_g, w_router_e, b_router_e, w_gate, w_up, w_down, g_post2):
    b, s, d = x.shape
    assert b == 1
    depth = w_ada.shape[0]
    slopes = jnp.asarray([2.0 ** (-8.0 * (i + 1) / N_HEADS) for i in range(N_HEADS)], F32)
    x2 = x.reshape(s, d)
    for l in range(depth):
        lam_init = 0.8 - 0.6 * math.exp(-0.3 * l)
        mod = _ada(c.reshape(d, 1), w_ada[l], b_ada[l].reshape(1, -1))
        sh1, sc1, gt1, sh2, sc2, gt2 = [mod[:, k * d:(k + 1) * d] for k in range(6)]
        proj = _inproj(x2, g_pre1[l].reshape(1, d), sc1, sh1, w_in[l].astype(BF16))
        y_attn = _attn(slopes, lam_q1[l].reshape(1, -1), lam_k1[l].reshape(1, -1),
                       lam_q2[l].reshape(1, -1), lam_k2[l].reshape(1, -1), proj,
                       g_subln[l].reshape(1, -1), lam_init)
        wr = jnp.concatenate([w_router_g[l], w_router_e[l].reshape(d, N_EXPERTS),
                              jnp.zeros((d, LANES - N_GROUPS - N_EXPERTS), F32)], axis=1).astype(BF16)
        br = jnp.concatenate([b_router_g[l], b_router_e[l].reshape(N_EXPERTS),
                              jnp.zeros((LANES - N_GROUPS - N_EXPERTS,), F32)]).reshape(1, LANES)
        x1, h2, rinfo = _outproj(y_attn, proj, x2, w_conv[l], w_out[l].astype(BF16),
                                 g_post1[l].reshape(1, d), gt1, g_pre2[l].reshape(1, d), sc2, sh2, wr, br)
        x2 = _moe(h2, rinfo, w_gate[l], w_up[l], w_down[l], x1, g_post2[l].reshape(1, d), gt2)
    return x2.reshape(b, s, d)
```

```python
import functools
import math

import jax
import jax.numpy as jnp
from jax import lax
from jax.experimental import pallas as pl
from jax.experimental.pallas import tpu as pltpu

F32 = jnp.float32
BF16 = jnp.bfloat16

LANES = 128
CHUNK = 64
N_HEADS = 8
QK_DIM = 64
V_DIM = 128
CONV_WIDTH = 3
N_GROUPS = 4
EXPERTS_PER_GROUP = 8
N_EXPERTS = N_GROUPS * EXPERTS_PER_GROUP
NORM_EPS = 1e-6
SUBLN_EPS = 1e-5
NEG_BIG = -1e30
VMEM_LIMIT = 56 * 1024 * 1024


def _rms(x, eps):
    return x * lax.rsqrt(jnp.mean(x * x, axis=-1, keepdims=True) + eps)


def _sigmoid(x):
    return 1.0 / (1.0 + jnp.exp(-x))


def _ada_kernel(c_ref, w_ref, b_ref, o_ref):
    c = c_ref[...]
    s = c * _sigmoid(c)
    o_ref[...] = jnp.sum(w_ref[...] * s, axis=0, keepdims=True) + b_ref[...]


def _ada(c_col, w_ada, b_ada, tn=1024):
    d, n = w_ada.shape
    return pl.pallas_call(
        _ada_kernel,
        out_shape=jax.ShapeDtypeStruct((1, n), F32),
        grid=(n // tn,),
        in_specs=[pl.BlockSpec((d, 1), lambda j: (0, 0)),
                  pl.BlockSpec((d, tn), lambda j: (0, j)),
                  pl.BlockSpec((1, tn), lambda j: (0, j))],
        out_specs=pl.BlockSpec((1, tn), lambda j: (0, j)),
        compiler_params=pltpu.CompilerParams(dimension_semantics=("arbitrary",),
                                             vmem_limit_bytes=VMEM_LIMIT),
        name="ada",
    )(c_col, w_ada, b_ada)


def _inproj_kernel(x_ref, g_ref, sc_ref, sh_ref, w_ref, o_ref, h_scr, *, cn):
    @pl.when(pl.program_id(1) == 0)
    def _():
        y = _rms(x_ref[...], NORM_EPS) * g_ref[...]
        h_scr[...] = (y * (1.0 + sc_ref[...]) + sh_ref[...]).astype(BF16)

    acc = jnp.dot(h_scr[...], w_ref[...], preferred_element_type=F32)
    for c in range(cn):
        o_ref[c] = acc[:, c * LANES:(c + 1) * LANES].astype(BF16)


def _inproj(x2, g, sc, sh, w_bf, tm=1024, cn=4):
    s, d = x2.shape
    n = w_bf.shape[1]
    nchunks = n // LANES
    row = lambda i, j: (0, 0)
    return pl.pallas_call(
        functools.partial(_inproj_kernel, cn=cn),
        out_shape=jax.ShapeDtypeStruct((nchunks, s, LANES), BF16),
        grid=(s // tm, nchunks // cn),
        in_specs=[pl.BlockSpec((tm, d), lambda i, j: (i, 0)),
                  pl.BlockSpec((1, d), row),
                  pl.BlockSpec((1, d), row),
                  pl.BlockSpec((1, d), row),
                  pl.BlockSpec((d, cn * LANES), lambda i, j: (0, j))],
        out_specs=pl.BlockSpec((cn, tm, LANES), lambda i, j: (j, i, 0)),
        scratch_shapes=[pltpu.VMEM((tm, d), BF16)],
        compiler_params=pltpu.CompilerParams(dimension_semantics=("arbitrary", "arbitrary"),
                                             vmem_limit_bytes=VMEM_LIMIT),
        name="inproj",
    )(x2, g, sc, sh, w_bf)


ATT_POS = 256
ATT_VROWS = 144


def _attn_kernel(slope_ref, lq1_ref, lk1_ref, lq2_ref, lk2_ref, q_ref, k_ref, v_ref, gsub_ref,
                 o_ref, vt_scr, sa_scr, sb_scr, m_scr, acc_scr, *, t, lam_init):
    h = pl.program_id(0)
    qi = pl.program_id(1)
    nkt = vt_scr.shape[0]

    @pl.when(qi == 0)
    def _():
        ones_row = jnp.where(lax.broadcasted_iota(jnp.int32, (ATT_VROWS - V_DIM, t), 0) == 0, 1.0, 0.0)

        def tr(c, carry):
            vc = v_ref[0, pl.ds(pl.multiple_of(c * t, t), t), :]
            vt_scr[c] = jnp.concatenate([vc.astype(F32).T, ones_row], axis=0).astype(BF16)
            return carry
        lax.fori_loop(0, nkt, tr, 0)

    slope = slope_ref[h]
    step_dec = slope * t
    qt = (q_ref[0].astype(F32) * (QK_DIM ** -0.5)).T
    row = lax.broadcasted_iota(jnp.int32, (LANES, t), 0)
    qfeat = jnp.where(row == 0, slope, jnp.where(row == 1, ATT_POS * slope, 0.0))
    qa = jnp.concatenate([jnp.where(row < QK_DIM, qt, 0.0), qfeat], axis=0)
    qb = jnp.concatenate([jnp.where(row >= QK_DIM, qt, 0.0), qfeat], axis=0)
    qts = jnp.concatenate([qa, qb], axis=1).astype(BF16)

    krow = lax.broadcasted_iota(jnp.int32, (t, LANES), 0)
    klane = lax.broadcasted_iota(jnp.int32, (t, LANES), 1)
    pbits = ATT_POS.bit_length() - 1
    kfeat = jnp.where(klane == 0, (krow & (ATT_POS - 1)).astype(F32),
                      jnp.where(klane == 1, (krow >> pbits).astype(F32), 0.0)).astype(BF16)

    m_scr[...] = jnp.full(m_scr.shape, NEG_BIG, F32)
    acc_scr[...] = jnp.zeros(acc_scr.shape, F32)

    def scores(j, s_ref):
        kj = k_ref[0, pl.ds(pl.multiple_of(j * t, t), t), :]
        s_ref[...] = jnp.dot(jnp.concatenate([kj, kfeat], axis=1), qts, preferred_element_type=F32)

    def colmax(s):
        while s.shape[0] > 8:
            half = s.shape[0] // 2
            s = jnp.maximum(s[:half], s[half:])
        return jnp.max(s, axis=0, keepdims=True)

    def softmax_pv(j, s_ref, corr):
        s = s_ref[...]
        if corr is not None:
            s = s + corr
        m_old = m_scr[...] - step_dec
        m_new = jnp.maximum(m_old, colmax(s))
        alpha = jnp.exp(m_old - m_new)
        p = jnp.exp(s - m_new).astype(BF16)
        pv = jnp.dot(vt_scr[j], p, preferred_element_type=F32)
        acc_scr[...] = alpha * acc_scr[...] + pv
        m_scr[...] = m_new

    def near(s_ref):
        krel = lax.broadcasted_iota(jnp.int32, (t, t), 0)
        qrel = lax.broadcasted_iota(jnp.int32, (t, t), 1)
        shift = CHUNK.bit_length() - 1
        allowed = (krel >> shift) <= (qrel >> shift)
        corr = jnp.where(allowed, slope * (qrel - jnp.abs(qrel - krel) - krel).astype(F32), NEG_BIG)
        softmax_pv(qi, s_ref, jnp.concatenate([corr, corr], axis=1))

    scores(0, sa_scr)

    def pair(i, carry):
        scores(2 * i + 1, sb_scr)
        softmax_pv(2 * i, sa_scr, None)
        scores(2 * i + 2, sa_scr)
        softmax_pv(2 * i + 1, sb_scr, None)
        return carry

    lax.fori_loop(0, qi // 2, pair, 0)

    @pl.when(qi % 2 == 1)
    def _():
        scores(qi, sb_scr)
        softmax_pv(qi - 1, sa_scr, None)
        near(sb_scr)

    @pl.when(qi % 2 == 0)
    def _():
        near(sa_scr)

    lam = (jnp.exp(jnp.sum(lq1_ref[...] * lk1_ref[...], axis=-1, keepdims=True))
           - jnp.exp(jnp.sum(lq2_ref[...] * lk2_ref[...], axis=-1, keepdims=True)) + lam_init)
    o1 = acc_scr[0:V_DIM, 0:t] / acc_scr[V_DIM:V_DIM + 1, 0:t]
    o2 = acc_scr[0:V_DIM, t:2 * t] / acc_scr[V_DIM:V_DIM + 1, t:2 * t]
    ot = o1 - lam * o2
    ot = ot * lax.rsqrt(jnp.mean(ot * ot, axis=0, keepdims=True) + SUBLN_EPS)
    o_ref[...] = (ot.T * gsub_ref[...] * (1.0 - lam_init)).astype(BF16)


def _attn(slopes, lq1, lk1, lq2, lk2, proj, gsub, lam_init, t=512):
    s = proj.shape[1]
    smem = pl.BlockSpec(memory_space=pltpu.SMEM)
    vec = lambda n: pl.BlockSpec((1, n), lambda h, i: (0, 0))
    return pl.pallas_call(
        functools.partial(_attn_kernel, t=t, lam_init=lam_init),
        out_shape=jax.ShapeDtypeStruct((s, N_HEADS * V_DIM), BF16),
        grid=(N_HEADS, s // t),
        in_specs=[smem, vec(QK_DIM), vec(QK_DIM), vec(QK_DIM), vec(QK_DIM),
                  pl.BlockSpec((1, t, LANES), lambda h, i: (h, i, 0)),
                  pl.BlockSpec((1, s, LANES), lambda h, i: (N_HEADS + h, 0, 0)),
                  pl.BlockSpec((1, s, LANES), lambda h, i: (2 * N_HEADS + h, 0, 0)),
                  vec(V_DIM)],
        out_specs=pl.BlockSpec((t, V_DIM), lambda h, i: (i, h)),
        scratch_shapes=[pltpu.VMEM((s // t, ATT_VROWS, t), BF16),
                        pltpu.VMEM((t, 2 * t), F32), pltpu.VMEM((t, 2 * t), F32),
                        pltpu.VMEM((1, 2 * t), F32),
                        pltpu.VMEM((ATT_VROWS, 2 * t), F32)],
        compiler_params=pltpu.CompilerParams(dimension_semantics=("arbitrary", "arbitrary"),
                                             vmem_limit_bytes=VMEM_LIMIT),
        name="attn",
    )(slopes, lq1, lk1, lq2, lk2, proj, proj, proj, gsub)


def _route(logits):
    lane = lax.broadcasted_iota(jnp.int32, logits.shape, 1).astype(F32)
    far = float(4 * LANES)
    gl = jnp.where(lane < N_GROUPS, logits, NEG_BIG)
    gmax = jnp.max(gl, axis=-1, keepdims=True)
    g_w = 1.0 / jnp.sum(jnp.exp(gl - gmax), axis=-1, keepdims=True)
    g_idx = jnp.min(jnp.where(gl == gmax, lane, far), axis=-1, keepdims=True)
    lo = N_GROUPS + EXPERTS_PER_GROUP * g_idx
    el = jnp.where((lane >= lo) & (lane < lo + EXPERTS_PER_GROUP), logits, NEG_BIG)
    e1 = jnp.max(el, axis=-1, keepdims=True)
    i1 = jnp.min(jnp.where(el == e1, lane, far), axis=-1, keepdims=True)
    el2 = jnp.where(lane == i1, NEG_BIG, el)
    e2 = jnp.max(el2, axis=-1, keepdims=True)
    i2 = jnp.min(jnp.where(el2 == e2, lane, far), axis=-1, keepdims=True)
    r = jnp.exp(e2 - e1)
    w1 = 1.0 / (1.0 + r)
    w2 = r * w1
    out = jnp.where(lane == 0.0, i1 - N_GROUPS, 0.0)
    out = jnp.where(lane == 1.0, i2 - N_GROUPS, out)
    out = jnp.where(lane == 2.0, g_w * w1, out)
    out = jnp.where(lane == 3.0, g_w * w2, out)
    return out


def _outproj_kernel(ya_ref, b_ref, c_ref, u_ref, x_ref, wconv_ref, wout_ref, gpost_ref, gt_ref,
                    gpre_ref, sc_ref, sh_ref, wr_ref, br_ref,
                    x1_ref, h2_ref, rinfo_ref, zbuf, ybuf, *, tm, nc):
    i = pl.program_id(0)
    halo = 8

    @pl.when(i == 0)
    def _():
        zbuf[0:halo, :] = jnp.zeros((halo, zbuf.shape[1]), F32)

    @pl.when(i > 0)
    def _():
        zbuf[0:halo, :] = zbuf[tm:tm + halo, :]

    for c in range(nc):
        cols = slice(c * LANES, (c + 1) * LANES)
        zbuf[halo:halo + tm, cols] = c_ref[c].astype(F32) * u_ref[c].astype(F32)
    d_attn = ya_ref.shape[1]
    ybuf[:, 0:d_attn] = ya_ref[...]
    for c in range(nc):
        cols = slice(c * LANES, (c + 1) * LANES)
        conv = (wconv_ref[0:1, cols] * zbuf[halo - 2:halo - 2 + tm, cols]
                + wconv_ref[1:2, cols] * zbuf[halo - 1:halo - 1 + tm, cols]
                + wconv_ref[2:3, cols] * zbuf[halo:halo + tm, cols])
        ybuf[:, d_attn + c * LANES:d_attn + (c + 1) * LANES] = (b_ref[c].astype(F32) * conv).astype(BF16)

    y = jnp.dot(ybuf[...], wout_ref[...], preferred_element_type=F32)
    x1 = x_ref[...] + gt_ref[...] * (_rms(y, NORM_EPS) * gpost_ref[...])
    x1_ref[...] = x1
    h2 = (_rms(x1, NORM_EPS) * gpre_ref[...]) * (1.0 + sc_ref[...]) + sh_ref[...]
    h2_ref[...] = h2
    logits = jnp.dot(h2.astype(BF16), wr_ref[...], preferred_element_type=F32) + br_ref[...]
    rinfo_ref[...] = _route(logits)


def _outproj(y_attn, proj, x2, w_conv, wout_bf, gpost, gt, gpre, sc, sh, wr_bf, br, tm=512):
    s, d = x2.shape
    d_attn = y_attn.shape[1]
    nc = (d - d_attn) // LANES
    base = 3 * d_attn // LANES
    row = lambda n: pl.BlockSpec((1, n), lambda i: (0, 0))
    blk = lambda g: pl.BlockSpec((nc, tm, LANES), lambda i, g=g: (base // nc + g, i, 0))
    return pl.pallas_call(
        functools.partial(_outproj_kernel, tm=tm, nc=nc),
        out_shape=(jax.ShapeDtypeStruct((s, d), F32),
                   jax.ShapeDtypeStruct((s, d), F32),
                   jax.ShapeDtypeStruct((s, LANES), F32)),
        grid=(s // tm,),
        in_specs=[pl.BlockSpec((tm, d_attn), lambda i: (i, 0)),
                  blk(0), blk(1), blk(2),
                  pl.BlockSpec((tm, d), lambda i: (i, 0)),
                  pl.BlockSpec((CONV_WIDTH, d - d_attn), lambda i: (0, 0)),
                  pl.BlockSpec((d, d), lambda i: (0, 0)),
                  row(d), row(d), row(d), row(d), row(d),
                  pl.BlockSpec((d, LANES), lambda i: (0, 0)),
                  row(LANES)],
        out_specs=(pl.BlockSpec((tm, d), lambda i: (i, 0)),
                   pl.BlockSpec((tm, d), lambda i: (i, 0)),
                   pl.BlockSpec((tm, LANES), lambda i: (i, 0))),
        scratch_shapes=[pltpu.VMEM((tm + 8, d - d_attn), F32), pltpu.VMEM((tm, d), BF16)],
        compiler_params=pltpu.CompilerParams(dimension_semantics=("arbitrary",),
                                             vmem_limit_bytes=VMEM_LIMIT),
        name="outproj",
    )(y_attn, proj, proj, proj, x2, w_conv, wout_bf, gpost, gt, gpre, sc, sh, wr_bf, br)


MOE_TILE = 256


def _moe_num_tiles(s):
    return (2 * s) // MOE_TILE + N_EXPERTS


def _plan_kernel(rinfo_ref, pos_ref, tinfo_ref, cnt_scr, base_scr, *, tm, ntm):
    ph = pl.program_id(0)
    i = pl.program_id(1)
    r = rinfo_ref[...]
    lane = lax.broadcasted_iota(jnp.int32, r.shape, 1).astype(F32)
    oh0 = jnp.where(lane == r[:, 0:1], 1.0, 0.0)
    oh1 = jnp.where(lane == r[:, 1:2], 1.0, 0.0)
    both = oh0 + oh1
    tile_cnt = jnp.sum(both, axis=0, keepdims=True)

    @pl.when((ph == 0) & (i == 0))
    def _():
        cnt_scr[...] = jnp.zeros_like(cnt_scr)

    @pl.when(ph == 0)
    def _():
        cnt_scr[...] += tile_cnt

    @pl.when((ph == 1) & (i == 0))
    def _():
        ktiles = jnp.floor((cnt_scr[...] + (MOE_TILE - 1)) * (1.0 / MOE_TILE))
        rr = lax.broadcasted_iota(jnp.int32, (LANES, LANES), 0)
        cc = lax.broadcasted_iota(jnp.int32, (LANES, LANES), 1)
        upper = jnp.where(rr < cc, 1.0, 0.0).astype(BF16)
        first = jnp.dot(jnp.broadcast_to(ktiles, (8, LANES)).astype(BF16), upper,
                        preferred_element_type=F32)[0:1, :]
        base_scr[...] = first * MOE_TILE
        ti = lax.broadcasted_iota(jnp.int32, (ntm, LANES), 0).astype(F32)
        el = lax.broadcasted_iota(jnp.int32, (ntm, LANES), 1).astype(F32)
        owned = jnp.where(ti >= first, jnp.where(ti < first + ktiles, 1.0, 0.0), 0.0)
        texp = jnp.sum(owned * el, axis=1, keepdims=True)
        nact = jnp.sum(ktiles, axis=1, keepdims=True)
        tinfo_ref[...] = jnp.where(el == 0.0, texp, jnp.where(el == 1.0, nact, 0.0))

    @pl.when(ph == 1)
    def _():
        rr = lax.broadcasted_iota(jnp.int32, (tm, tm), 0)
        cc = lax.broadcasted_iota(jnp.int32, (tm, tm), 1)
        lower = jnp.where(cc < rr, 1.0, 0.0).astype(BF16)
        before = jnp.dot(lower, both.astype(BF16), preferred_element_type=F32)
        val = before + base_scr[...]
        p0 = jnp.sum(oh0 * val, axis=1, keepdims=True)
        p1 = jnp.sum(oh1 * val, axis=1, keepdims=True)
        pos_ref[...] = jnp.where(lane == 0.0, p0, jnp.where(lane == 1.0, p1, 0.0))
        base_scr[...] += tile_cnt


def _plan(rinfo, tm=256):
    s = rinfo.shape[0]
    ntm = _moe_num_tiles(s)
    return pl.pallas_call(
        functools.partial(_plan_kernel, tm=tm, ntm=ntm),
        out_shape=(jax.ShapeDtypeStruct((s, LANES), F32), jax.ShapeDtypeStruct((ntm, LANES), F32)),
        grid=(2, s // tm),
        in_specs=[pl.BlockSpec((tm, LANES), lambda p, i: (i, 0))],
        out_specs=(pl.BlockSpec((tm, LANES), lambda p, i: (i * p, 0)),
                   pl.BlockSpec((ntm, LANES), lambda p, i: (0, 0))),
        scratch_shapes=[pltpu.VMEM((1, LANES), F32), pltpu.VMEM((1, LANES), F32)],
        compiler_params=pltpu.CompilerParams(dimension_semantics=("arbitrary", "arbitrary"),
                                             vmem_limit_bytes=VMEM_LIMIT),
        name="plan",
    )(rinfo)


def _dispatch_kernel(pos_ref, texp_ref, nact_ref, h2_ref, xs_ref, zero_scr, sem, zsem, *, tm):
    i = pl.program_id(0)

    @pl.when(i == 0)
    def _():
        zero_scr[...] = jnp.zeros_like(zero_scr)
        nact = nact_ref[0]
        ntm = xs_ref.shape[0] // MOE_TILE

        def is_last(t):
            nxt = texp_ref[jnp.minimum(t + 1, nact - 1)]
            return (t >= nact - 1) | (nxt != texp_ref[jnp.minimum(t, nact - 1)])

        def zstart(t, carry):
            @pl.when(is_last(t))
            def _():
                pltpu.make_async_copy(zero_scr, xs_ref.at[pl.ds(pl.multiple_of(t * MOE_TILE, MOE_TILE), MOE_TILE), :],
                                      zsem).start()
            return carry

        def zwait(t, carry):
            @pl.when(is_last(t))
            def _():
                pltpu.make_async_copy(zero_scr, xs_ref.at[pl.ds(0, MOE_TILE), :], zsem).wait()
            return carry

        lax.fori_loop(0, ntm, zstart, 0)
        lax.fori_loop(0, ntm, zwait, 0)

    def start(r, carry):
        tok = i * tm + r
        for slot in range(2):
            p = pos_ref[2 * tok + slot]
            pltpu.make_async_copy(h2_ref.at[pl.ds(r, 1), :], xs_ref.at[pl.ds(p, 1), :], sem).start()
        return carry

    def wait(r, carry):
        for slot in range(2):
            pltpu.make_async_copy(h2_ref.at[pl.ds(0, 1), :], xs_ref.at[pl.ds(0, 1), :], sem).wait()
        return carry

    lax.fori_loop(0, tm, start, 0)
    lax.fori_loop(0, tm, wait, 0)


def _dispatch(pos_flat, texp, nact, h2, tm=256):
    s, d = h2.shape
    ntm = _moe_num_tiles(s)
    return pl.pallas_call(
        functools.partial(_dispatch_kernel, tm=tm),
        out_shape=jax.ShapeDtypeStruct((ntm * MOE_TILE, d), F32),
        grid_spec=pltpu.PrefetchScalarGridSpec(
            num_scalar_prefetch=3,
            grid=(s // tm,),
            in_specs=[pl.BlockSpec((tm, d), lambda i, *_: (i, 0))],
            out_specs=pl.BlockSpec(memory_space=pl.ANY),
            scratch_shapes=[pltpu.VMEM((MOE_TILE, d), F32), pltpu.SemaphoreType.DMA, pltpu.SemaphoreType.DMA]),
        compiler_params=pltpu.CompilerParams(dimension_semantics=("arbitrary",),
                                             vmem_limit_bytes=VMEM_LIMIT),
        name="dispatch",
    )(pos_flat, texp, nact, h2)


def _experts_kernel(texp_ref, nact_ref, xs_ref, wg_ref, wu_ref, wd_ref, ys_ref, wg_bf, wu_bf, wd_bf):
    i = pl.program_id(0)
    nact = nact_ref[0]
    j = jnp.minimum(i, nact - 1)
    fresh = (i == 0) | (texp_ref[j] != texp_ref[jnp.maximum(j - 1, 0)])

    @pl.when((i < nact) & fresh)
    def _():
        wg_bf[...] = wg_ref[0].astype(BF16)
        wu_bf[...] = wu_ref[0].astype(BF16)
        wd_bf[...] = wd_ref[0].astype(BF16)

    @pl.when(i < nact)
    def _():
        x = xs_ref[...].astype(BF16)
        hg = jnp.dot(x, wg_bf[...], preferred_element_type=F32)
        hu = jnp.dot(x, wu_bf[...], preferred_element_type=F32)
        a = (hg * _sigmoid(hg)) * hu
        ys_ref[...] = jnp.dot(a.astype(BF16), wd_bf[...], preferred_element_type=F32)

    @pl.when(i >= nact)
    def _():
        ys_ref[...] = jnp.zeros_like(ys_ref)


def _experts(texp, nact, xs, w_gate, w_up, w_down):
    p, d = xs.shape
    ntm = p // MOE_TILE
    _, _, f = w_gate.shape

    def tile(i, texp_ref, nact_ref):
        return (jnp.minimum(i, nact_ref[0] - 1), 0)

    def wsel(i, texp_ref, nact_ref):
        return (texp_ref[jnp.minimum(i, nact_ref[0] - 1)], 0, 0)

    return pl.pallas_call(
        _experts_kernel,
        out_shape=jax.ShapeDtypeStruct((p, d), F32),
        grid_spec=pltpu.PrefetchScalarGridSpec(
            num_scalar_prefetch=2,
            grid=(ntm,),
            in_specs=[pl.BlockSpec((MOE_TILE, d), tile),
                      pl.BlockSpec((1, d, f), wsel),
                      pl.BlockSpec((1, d, f), wsel),
                      pl.BlockSpec((1, f, d), wsel)],
            out_specs=pl.BlockSpec((MOE_TILE, d), lambda i, *_: (i, 0)),
            scratch_shapes=[pltpu.VMEM((d, f), BF16), pltpu.VMEM((d, f), BF16), pltpu.VMEM((f, d), BF16)]),
        compiler_params=pltpu.CompilerParams(dimension_semantics=("arbitrary",),
                                             vmem_limit_bytes=VMEM_LIMIT),
        name="experts",
    )(texp, nact, xs, w_gate, w_up, w_down)


def _combine_kernel(pos_ref, ys_ref, rinfo_ref, x1_ref, gpost_ref, gt_ref, o_ref, ybuf, sem, *, tm):
    i = pl.program_id(0)

    def start(r, carry):
        tok = i * tm + r
        for slot in range(2):
            p = pos_ref[2 * tok + slot]
            pltpu.make_async_copy(ys_ref.at[pl.ds(p, 1), :], ybuf.at[slot, pl.ds(r, 1), :], sem).start()
        return carry

    def wait(r, carry):
        for slot in range(2):
            pltpu.make_async_copy(ys_ref.at[pl.ds(0, 1), :], ybuf.at[slot, pl.ds(0, 1), :], sem).wait()
        return carry

    lax.fori_loop(0, tm, start, 0)
    lax.fori_loop(0, tm, wait, 0)
    r = rinfo_ref[...]
    y = r[:, 2:3] * ybuf[0] + r[:, 3:4] * ybuf[1]
    o_ref[...] = x1_ref[...] + gt_ref[...] * (_rms(y, NORM_EPS) * gpost_ref[...])


def _combine(pos_flat, ys, rinfo, x1, gpost, gt, tm=256):
    s, d = x1.shape
    row = pl.BlockSpec((1, d), lambda i, *_: (0, 0))
    return pl.pallas_call(
        functools.partial(_combine_kernel, tm=tm),
        out_shape=jax.ShapeDtypeStruct((s, d), F32),
        grid_spec=pltpu.PrefetchScalarGridSpec(
            num_scalar_prefetch=1,
            grid=(s // tm,),
            in_specs=[pl.BlockSpec(memory_space=pl.ANY),
                      pl.BlockSpec((tm, LANES), lambda i, *_: (i, 0)),
                      pl.BlockSpec((tm, d), lambda i, *_: (i, 0)),
                      row, row],
            out_specs=pl.BlockSpec((tm, d), lambda i, *_: (i, 0)),
            scratch_shapes=[pltpu.VMEM((2, tm, d), F32), pltpu.SemaphoreType.DMA]),
        compiler_params=pltpu.CompilerParams(dimension_semantics=("arbitrary",),
                                             vmem_limit_bytes=VMEM_LIMIT),
        name="combine",
    )(pos_flat, ys, rinfo, x1, gpost, gt)


def _moe(h2, rinfo, w_gate, w_up, w_down, x1, gpost, gt):
    pos, tinfo = _plan(rinfo)
    pos_flat = pos[:, 0:2].astype(jnp.int32).reshape(-1)
    texp = tinfo[:, 0].astype(jnp.int32)
    nact = tinfo[0:1, 1].astype(jnp.int32)
    xs = _dispatch(pos_flat, texp, nact, h2)
    ys = _experts(texp, nact, xs, w_gate, w_up, w_down)
    return _combine(pos_flat, ys, rinfo, x1, gpost, gt)


def kernel(x, c, w_ada, b_ada, g_pre1, w_in, lam_q1, lam_k1, lam_q2, lam_k2, g_subln, w_conv, w_out,
           g_post1, g_pre2, w_router_g, b_router_g, w_router_e, b_router_e, w_gate, w_up, w_down, g_post2):
    b, s, d = x.shape
    assert b == 1
    depth = w_ada.shape[0]
    slopes = jnp.asarray([2.0 ** (-8.0 * (i + 1) / N_HEADS) for i in range(N_HEADS)], F32)
    x2 = x.reshape(s, d)
    for l in range(depth):
        lam_init = 0.8 - 0.6 * math.exp(-0.3 * l)
        mod = _ada(c.reshape(d, 1), w_ada[l], b_ada[l].reshape(1, -1))
        sh1, sc1, gt1, sh2, sc2, gt2 = [mod[:, k * d:(k + 1) * d] for k in range(6)]
        proj = _inproj(x2, g_pre1[l].reshape(1, d), sc1, sh1, w_in[l].astype(BF16))
        y_attn = _attn(slopes, lam_q1[l].reshape(1, -1), lam_k1[l].reshape(1, -1),
                       lam_q2[l].reshape(1, -1), lam_k2[l].reshape(1, -1), proj,
                       g_subln[l].reshape(1, -1), lam_init)
        wr = jnp.concatenate([w_router_g[l], w_router_e[l].reshape(d, N_EXPERTS),
                              jnp.zeros((d, LANES - N_GROUPS - N_EXPERTS), F32)], axis=1).astype(BF16)
        br = jnp.concatenate([b_router_g[l], b_router_e[l].reshape(N_EXPERTS),
                              jnp.zeros((LANES - N_GROUPS - N_EXPERTS,), F32)]).reshape(1, LANES)
        x1, h2, rinfo = _outproj(y_attn, proj, x2, w_conv[l], w_out[l].astype(BF16),
                                 g_post1[l].reshape(1, d), gt1, g_pre2[l].reshape(1, d), sc2, sh2, wr, br)
        x2 = _moe(h2, rinfo, w_gate[l], w_up[l], w_down[l], x1, g_post2[l].reshape(1, d), gt2)
    return x2.reshape(b, s, d)
```

```python
import functools
import math

import jax
import jax.numpy as jnp
from jax import lax
from jax.experimental import pallas as pl
from jax.experimental.pallas import tpu as pltpu

F32 = jnp.float32
BF16 = jnp.bfloat16

LANES = 128
CHUNK = 64
N_HEADS = 8
QK_DIM = 64
V_DIM = 128
CONV_WIDTH = 3
N_GROUPS = 4
EXPERTS_PER_GROUP = 8
N_EXPERTS = N_GROUPS * EXPERTS_PER_GROUP
NORM_EPS = 1e-6
SUBLN_EPS = 1e-5
NEG_BIG = -1e30
LOG2E = 1.4426950408889634
VMEM_LIMIT = 56 * 1024 * 1024


def _rms(x, eps):
    return x * lax.rsqrt(jnp.mean(x * x, axis=-1, keepdims=True) + eps)


def _sigmoid(x):
    return 1.0 / (1.0 + jnp.exp(-x))


def _ada_kernel(c_ref, w_ref, b_ref, o_ref):
    c = c_ref[...]
    s = c * _sigmoid(c)
    o_ref[...] = jnp.sum(w_ref[...] * s, axis=0, keepdims=True) + b_ref[...]


def _ada(c_col, w_ada, b_ada, tn=1024):
    d, n = w_ada.shape
    return pl.pallas_call(
        _ada_kernel,
        out_shape=jax.ShapeDtypeStruct((1, n), F32),
        grid=(n // tn,),
        in_specs=[pl.BlockSpec((d, 1), lambda j: (0, 0)),
                  pl.BlockSpec((d, tn), lambda j: (0, j)),
                  pl.BlockSpec((1, tn), lambda j: (0, j))],
        out_specs=pl.BlockSpec((1, tn), lambda j: (0, j)),
        compiler_params=pltpu.CompilerParams(dimension_semantics=("arbitrary",),
                                             vmem_limit_bytes=VMEM_LIMIT),
        name="ada",
    )(c_col, w_ada, b_ada)


def _inproj_kernel(x_ref, g_ref, sc_ref, sh_ref, w_ref, o_ref, h_scr, *, cn):
    @pl.when(pl.program_id(1) == 0)
    def _():
        y = _rms(x_ref[...], NORM_EPS) * g_ref[...]
        h_scr[...] = (y * (1.0 + sc_ref[...]) + sh_ref[...]).astype(BF16)

    acc = jnp.dot(h_scr[...], w_ref[...], preferred_element_type=F32)
    for c in range(cn):
        o_ref[c] = acc[:, c * LANES:(c + 1) * LANES].astype(BF16)


def _inproj(x2, g, sc, sh, w_bf, tm=1024, cn=4):
    s, d = x2.shape
    n = w_bf.shape[1]
    nchunks = n // LANES
    row = lambda i, j: (0, 0)
    return pl.pallas_call(
        functools.partial(_inproj_kernel, cn=cn),
        out_shape=jax.ShapeDtypeStruct((nchunks, s, LANES), BF16),
        grid=(s // tm, nchunks // cn),
        in_specs=[pl.BlockSpec((tm, d), lambda i, j: (i, 0)),
                  pl.BlockSpec((1, d), row),
                  pl.BlockSpec((1, d), row),
                  pl.BlockSpec((1, d), row),
                  pl.BlockSpec((d, cn * LANES), lambda i, j: (0, j))],
        out_specs=pl.BlockSpec((cn, tm, LANES), lambda i, j: (j, i, 0)),
        scratch_shapes=[pltpu.VMEM((tm, d), BF16)],
        compiler_params=pltpu.CompilerParams(dimension_semantics=("arbitrary", "arbitrary"),
                                             vmem_limit_bytes=VMEM_LIMIT),
        name="inproj",
    )(x2, g, sc, sh, w_bf)


ATT_POS = 256
ATT_VROWS = 144


def _attn_kernel(slope_ref, lq1_ref, lk1_ref, lq2_ref, lk2_ref, q_ref, k_ref, v_ref, gsub_ref,
                 o_ref, vt_scr, sa_scr, sb_scr, m_scr, acc_scr, corr_scr, *, t, lam_init):
    h = pl.program_id(0)
    qi = pl.program_id(1)
    nkt = vt_scr.shape[0]
    slope2 = slope_ref[h] * LOG2E
    c_hi = jnp.full((LANES, t), slope2, F32).astype(BF16).astype(F32)
    c_lo = (jnp.full((LANES, t), slope2, F32) - c_hi).astype(BF16).astype(F32)

    @pl.when(qi == 0)
    def _():
        ones_row = jnp.where(lax.broadcasted_iota(jnp.int32, (ATT_VROWS - V_DIM, t), 0) == 0, 1.0, 0.0)

        def tr(c, carry):
            vc = v_ref[0, pl.ds(pl.multiple_of(c * t, t), t), :]
            vt_scr[c] = jnp.concatenate([vc.astype(F32).T, ones_row], axis=0).astype(BF16)
            return carry
        lax.fori_loop(0, nkt, tr, 0)

        krel = lax.broadcasted_iota(jnp.int32, (t, t), 0)
        qrel = lax.broadcasted_iota(jnp.int32, (t, t), 1)
        shift = CHUNK.bit_length() - 1
        allowed = (krel >> shift) <= (qrel >> shift)
        c_mxu = (c_hi + c_lo)[0:1, 0:1]
        corr_scr[...] = jnp.where(allowed, slope2 * (qrel - jnp.abs(qrel - krel)).astype(F32)
                                  - c_mxu * krel.astype(F32), NEG_BIG)

    step_dec = slope2 * t
    qt = (q_ref[0].astype(F32) * (LOG2E * QK_DIM ** -0.5)).T
    row = lax.broadcasted_iota(jnp.int32, (LANES, t), 0)
    qfeat = jnp.where(row == 0, c_hi, jnp.where(row == 1, ATT_POS * c_hi,
                      jnp.where(row == 2, c_lo, jnp.where(row == 3, ATT_POS * c_lo, 0.0))))
    qa = jnp.concatenate([jnp.where(row < QK_DIM, qt, 0.0), qfeat], axis=0)
    qb = jnp.concatenate([jnp.where(row >= QK_DIM, qt, 0.0), qfeat], axis=0)
    qts = jnp.concatenate([qa, qb], axis=1).astype(BF16)

    krow = lax.broadcasted_iota(jnp.int32, (t, LANES), 0)
    klane = lax.broadcasted_iota(jnp.int32, (t, LANES), 1)
    pbits = ATT_POS.bit_length() - 1
    kmod = (krow & (ATT_POS - 1)).astype(F32)
    kdiv = (krow >> pbits).astype(F32)
    kfeat = jnp.where((klane == 0) | (klane == 2), kmod,
                      jnp.where((klane == 1) | (klane == 3), kdiv, 0.0)).astype(BF16)

    m_scr[...] = jnp.full(m_scr.shape, NEG_BIG, F32)
    acc_scr[...] = jnp.zeros(acc_scr.shape, F32)

    def scores(j, s_ref):
        kj = k_ref[0, pl.ds(pl.multiple_of(j * t, t), t), :]
        s_ref[...] = jnp.dot(jnp.concatenate([kj, kfeat], axis=1), qts, preferred_element_type=F32)

    def colmax(s):
        while s.shape[0] > 8:
            half = s.shape[0] // 2
            s = jnp.maximum(s[:half], s[half:])
        return jnp.max(s, axis=0, keepdims=True)

    def softmax_pv(j, s_ref, corr):
        s = s_ref[...]
        if corr is not None:
            s = s + corr
        m_old = m_scr[...] - step_dec
        m_new = jnp.maximum(m_old, colmax(s))
        alpha = jnp.exp2(m_old - m_new)
        p = jnp.exp2(s - m_new).astype(BF16)
        pv = jnp.dot(vt_scr[j], p, preferred_element_type=F32)
        acc_scr[...] = alpha * acc_scr[...] + pv
        m_scr[...] = m_new

    def near(s_ref):
        corr = corr_scr[...]
        softmax_pv(qi, s_ref, jnp.concatenate([corr, corr], axis=1))

    scores(0, sa_scr)

    def pair(i, carry):
        scores(2 * i + 1, sb_scr)
        softmax_pv(2 * i, sa_scr, None)
        scores(2 * i + 2, sa_scr)
        softmax_pv(2 * i + 1, sb_scr, None)
        return carry

    def quad(i, carry):
        return pair(2 * i + 1, pair(2 * i, carry))

    lax.fori_loop(0, qi // 4, quad, 0)
    lax.fori_loop(2 * (qi // 4), qi // 2, pair, 0)

    @pl.when(qi % 2 == 1)
    def _():
        scores(qi, sb_scr)
        softmax_pv(qi - 1, sa_scr, None)
        near(sb_scr)

    @pl.when(qi % 2 == 0)
    def _():
        near(sa_scr)

    lam = (jnp.exp(jnp.sum(lq1_ref[...] * lk1_ref[...], axis=-1, keepdims=True))
           - jnp.exp(jnp.sum(lq2_ref[...] * lk2_ref[...], axis=-1, keepdims=True)) + lam_init)
    o1 = acc_scr[0:V_DIM, 0:t] / acc_scr[V_DIM:V_DIM + 1, 0:t]
    o2 = acc_scr[0:V_DIM, t:2 * t] / acc_scr[V_DIM:V_DIM + 1, t:2 * t]
    ot = o1 - lam * o2
    ot = ot * lax.rsqrt(jnp.mean(ot * ot, axis=0, keepdims=True) + SUBLN_EPS)
    o_ref[...] = (ot.T * gsub_ref[...] * (1.0 - lam_init)).astype(BF16)


def _attn(slopes, lq1, lk1, lq2, lk2, proj, gsub, lam_init, t=512):
    s = proj.shape[1]
    smem = pl.BlockSpec(memory_space=pltpu.SMEM)
    vec = lambda n: pl.BlockSpec((1, n), lambda h, i: (0, 0))
    return pl.pallas_call(
        functools.partial(_attn_kernel, t=t, lam_init=lam_init),
        out_shape=jax.ShapeDtypeStruct((s, N_HEADS * V_DIM), BF16),
        grid=(N_HEADS, s // t),
        in_specs=[smem, vec(QK_DIM), vec(QK_DIM), vec(QK_DIM), vec(QK_DIM),
                  pl.BlockSpec((1, t, LANES), lambda h, i: (h, i, 0)),
                  pl.BlockSpec((1, s, LANES), lambda h, i: (N_HEADS + h, 0, 0)),
                  pl.BlockSpec((1, s, LANES), lambda h, i: (2 * N_HEADS + h, 0, 0)),
                  vec(V_DIM)],
        out_specs=pl.BlockSpec((t, V_DIM), lambda h, i: (i, h)),
        scratch_shapes=[pltpu.VMEM((s // t, ATT_VROWS, t), BF16),
                        pltpu.VMEM((t, 2 * t), F32), pltpu.VMEM((t, 2 * t), F32),
                        pltpu.VMEM((1, 2 * t), F32),
                        pltpu.VMEM((ATT_VROWS, 2 * t), F32),
                        pltpu.VMEM((t, t), F32)],
        compiler_params=pltpu.CompilerParams(dimension_semantics=("arbitrary", "arbitrary"),
                                             vmem_limit_bytes=VMEM_LIMIT),
        name="attn",
    )(slopes, lq1, lk1, lq2, lk2, proj, proj, proj, gsub)


def _route(logits):
    lane = lax.broadcasted_iota(jnp.int32, logits.shape, 1).astype(F32)
    far = float(4 * LANES)
    gl = jnp.where(lane < N_GROUPS, logits, NEG_BIG)
    gmax = jnp.max(gl, axis=-1, keepdims=True)
    g_w = 1.0 / jnp.sum(jnp.exp(gl - gmax), axis=-1, keepdims=True)
    g_idx = jnp.min(jnp.where(gl == gmax, lane, far), axis=-1, keepdims=True)
    lo = N_GROUPS + EXPERTS_PER_GROUP * g_idx
    el = jnp.where((lane >= lo) & (lane < lo + EXPERTS_PER_GROUP), logits, NEG_BIG)
    e1 = jnp.max(el, axis=-1, keepdims=True)
    i1 = jnp.min(jnp.where(el == e1, lane, far), axis=-1, keepdims=True)
    el2 = jnp.where(lane == i1, NEG_BIG, el)
    e2 = jnp.max(el2, axis=-1, keepdims=True)
    i2 = jnp.min(jnp.where(el2 == e2, lane, far), axis=-1, keepdims=True)
    r = jnp.exp(e2 - e1)
    w1 = 1.0 / (1.0 + r)
    w2 = r * w1
    out = jnp.where(lane == 0.0, i1 - N_GROUPS, 0.0)
    out = jnp.where(lane == 1.0, i2 - N_GROUPS, out)
    out = jnp.where(lane == 2.0, g_w * w1, out)
    out = jnp.where(lane == 3.0, g_w * w2, out)
    return out


def _outproj_kernel(ya_ref, b_ref, c_ref, u_ref, x_ref, wconv_ref, wout_ref, gpost_ref, gt_ref,
                    gpre_ref, sc_ref, sh_ref, wr_ref, br_ref,
                    x1_ref, h2_ref, rinfo_ref, zbuf, ybuf, *, tm, nc):
    i = pl.program_id(0)
    halo = 8

    @pl.when(i == 0)
    def _():
        zbuf[0:halo, :] = jnp.zeros((halo, zbuf.shape[1]), F32)

    @pl.when(i > 0)
    def _():
        zbuf[0:halo, :] = zbuf[tm:tm + halo, :]

    for c in range(nc):
        cols = slice(c * LANES, (c + 1) * LANES)
        zbuf[halo:halo + tm, cols] = c_ref[c].astype(F32) * u_ref[c].astype(F32)
    d_attn = ya_ref.shape[1]
    ybuf[:, 0:d_attn] = ya_ref[...]
    for c in range(nc):
        cols = slice(c * LANES, (c + 1) * LANES)
        conv = (wconv_ref[0:1, cols] * zbuf[halo - 2:halo - 2 + tm, cols]
                + wconv_ref[1:2, cols] * zbuf[halo - 1:halo - 1 + tm, cols]
                + wconv_ref[2:3, cols] * zbuf[halo:halo + tm, cols])
        ybuf[:, d_attn + c * LANES:d_attn + (c + 1) * LANES] = (b_ref[c].astype(F32) * conv).astype(BF16)

    y = jnp.dot(ybuf[...], wout_ref[...], preferred_element_type=F32)
    x1 = x_ref[...] + gt_ref[...] * (_rms(y, NORM_EPS) * gpost_ref[...])
    x1_ref[...] = x1
    h2 = (_rms(x1, NORM_EPS) * gpre_ref[...]) * (1.0 + sc_ref[...]) + sh_ref[...]
    h2_ref[...] = h2
    logits = jnp.dot(h2.astype(BF16), wr_ref[...], preferred_element_type=F32) + br_ref[...]
    rinfo_ref[...] = _route(logits)


def _outproj(y_attn, proj, x2, w_conv, wout_bf, gpost, gt, gpre, sc, sh, wr_bf, br, tm=512):
    s, d = x2.shape
    d_attn = y_attn.shape[1]
    nc = (d - d_attn) // LANES
    base = 3 * d_attn // LANES
    row = lambda n: pl.BlockSpec((1, n), lambda i: (0, 0))
    blk = lambda g: pl.BlockSpec((nc, tm, LANES), lambda i, g=g: (base // nc + g, i, 0))
    return pl.pallas_call(
        functools.partial(_outproj_kernel, tm=tm, nc=nc),
        out_shape=(jax.ShapeDtypeStruct((s, d), F32),
                   jax.ShapeDtypeStruct((s, d), F32),
                   jax.ShapeDtypeStruct((s, LANES), F32)),
        grid=(s // tm,),
        in_specs=[pl.BlockSpec((tm, d_attn), lambda i: (i, 0)),
                  blk(0), blk(1), blk(2),
                  pl.BlockSpec((tm, d), lambda i: (i, 0)),
                  pl.BlockSpec((CONV_WIDTH, d - d_attn), lambda i: (0, 0)),
                  pl.BlockSpec((d, d), lambda i: (0, 0)),
                  row(d), row(d), row(d), row(d), row(d),
                  pl.BlockSpec((d, LANES), lambda i: (0, 0)),
                  row(LANES)],
        out_specs=(pl.BlockSpec((tm, d), lambda i: (i, 0)),
                   pl.BlockSpec((tm, d), lambda i: (i, 0)),
                   pl.BlockSpec((tm, LANES), lambda i: (i, 0))),
        scratch_shapes=[pltpu.VMEM((tm + 8, d - d_attn), F32), pltpu.VMEM((tm, d), BF16)],
        compiler_params=pltpu.CompilerParams(dimension_semantics=("arbitrary",),
                                             vmem_limit_bytes=VMEM_LIMIT),
        name="outproj",
    )(y_attn, proj, proj, proj, x2, w_conv, wout_bf, gpost, gt, gpre, sc, sh, wr_bf, br)


MOE_TILE = 256


def _moe_num_tiles(s):
    return (2 * s) // MOE_TILE + N_EXPERTS


def _plan_kernel(rinfo_ref, pos_ref, tinfo_ref, cnt_scr, base_scr, *, tm, ntm):
    ph = pl.program_id(0)
    i = pl.program_id(1)
    r = rinfo_ref[...]
    lane = lax.broadcasted_iota(jnp.int32, r.shape, 1).astype(F32)
    oh0 = jnp.where(lane == r[:, 0:1], 1.0, 0.0)
    oh1 = jnp.where(lane == r[:, 1:2], 1.0, 0.0)
    both = oh0 + oh1
    tile_cnt = jnp.sum(both, axis=0, keepdims=True)

    @pl.when((ph == 0) & (i == 0))
    def _():
        cnt_scr[...] = jnp.zeros_like(cnt_scr)

    @pl.when(ph == 0)
    def _():
        cnt_scr[...] += tile_cnt

    @pl.when((ph == 1) & (i == 0))
    def _():
        ktiles = jnp.floor((cnt_scr[...] + (MOE_TILE - 1)) * (1.0 / MOE_TILE))
        rr = lax.broadcasted_iota(jnp.int32, (LANES, LANES), 0)
        cc = lax.broadcasted_iota(jnp.int32, (LANES, LANES), 1)
        upper = jnp.where(rr < cc, 1.0, 0.0).astype(BF16)
        first = jnp.dot(jnp.broadcast_to(ktiles, (8, LANES)).astype(BF16), upper,
                        preferred_element_type=F32)[0:1, :]
        base_scr[...] = first * MOE_TILE
        ti = lax.broadcasted_iota(jnp.int32, (ntm, LANES), 0).astype(F32)
        el = lax.broadcasted_iota(jnp.int32, (ntm, LANES), 1).astype(F32)
        owned = jnp.where(ti >= first, jnp.where(ti < first + ktiles, 1.0, 0.0), 0.0)
        texp = jnp.sum(owned * el, axis=1, keepdims=True)
        nact = jnp.sum(ktiles, axis=1, keepdims=True)
        tinfo_ref[...] = jnp.where(el == 0.0, texp, jnp.where(el == 1.0, nact, 0.0))

    @pl.when(ph == 1)
    def _():
        rr = lax.broadcasted_iota(jnp.int32, (tm, tm), 0)
        cc = lax.broadcasted_iota(jnp.int32, (tm, tm), 1)
        lower = jnp.where(cc < rr, 1.0, 0.0).astype(BF16)
        before = jnp.dot(lower, both.astype(BF16), preferred_element_type=F32)
        val = before + base_scr[...]
        p0 = jnp.sum(oh0 * val, axis=1, keepdims=True)
        p1 = jnp.sum(oh1 * val, axis=1, keepdims=True)
        pos_ref[...] = jnp.where(lane == 0.0, p0, jnp.where(lane == 1.0, p1, 0.0))
        base_scr[...] += tile_cnt


def _plan(rinfo, tm=256):
    s = rinfo.shape[0]
    ntm = _moe_num_tiles(s)
    return pl.pallas_call(
        functools.partial(_plan_kernel, tm=tm, ntm=ntm),
        out_shape=(jax.ShapeDtypeStruct((s, LANES), F32), jax.ShapeDtypeStruct((ntm, LANES), F32)),
        grid=(2, s // tm),
        in_specs=[pl.BlockSpec((tm, LANES), lambda p, i: (i, 0))],
        out_specs=(pl.BlockSpec((tm, LANES), lambda p, i: (i * p, 0)),
                   pl.BlockSpec((ntm, LANES), lambda p, i: (0, 0))),
        scratch_shapes=[pltpu.VMEM((1, LANES), F32), pltpu.VMEM((1, LANES), F32)],
        compiler_params=pltpu.CompilerParams(dimension_semantics=("arbitrary", "arbitrary"),
                                             vmem_limit_bytes=VMEM_LIMIT),
        name="plan",
    )(rinfo)


def _dispatch_kernel(pos_ref, texp_ref, nact_ref, h2_ref, xs_ref, zero_scr, sem, zsem, *, tm):
    i = pl.program_id(0)

    @pl.when(i == 0)
    def _():
        zero_scr[...] = jnp.zeros_like(zero_scr)
        nact = nact_ref[0]
        ntm = xs_ref.shape[0] // MOE_TILE

        def is_last(t):
            nxt = texp_ref[jnp.minimum(t + 1, nact - 1)]
            return (t >= nact - 1) | (nxt != texp_ref[jnp.minimum(t, nact - 1)])

        def zstart(t, carry):
            @pl.when(is_last(t))
            def _():
                pltpu.make_async_copy(zero_scr, xs_ref.at[pl.ds(pl.multiple_of(t * MOE_TILE, MOE_TILE), MOE_TILE), :],
                                      zsem).start()
            return carry

        def zwait(t, carry):
            @pl.when(is_last(t))
            def _():
                pltpu.make_async_copy(zero_scr, xs_ref.at[pl.ds(0, MOE_TILE), :], zsem).wait()
            return carry

        lax.fori_loop(0, ntm, zstart, 0)
        lax.fori_loop(0, ntm, zwait, 0)

    def start(r, carry):
        tok = i * tm + r
        for slot in range(2):
            p = pos_ref[2 * tok + slot]
            pltpu.make_async_copy(h2_ref.at[pl.ds(r, 1), :], xs_ref.at[pl.ds(p, 1), :], sem).start()
        return carry

    lax.fori_loop(0, tm, start, 0, unroll=8)
    for slot in range(2):
        pltpu.make_async_copy(h2_ref, xs_ref.at[pl.ds(0, tm), :], sem).wait()


def _dispatch(pos_flat, texp, nact, h2, tm=256):
    s, d = h2.shape
    ntm = _moe_num_tiles(s)
    return pl.pallas_call(
        functools.partial(_dispatch_kernel, tm=tm),
        out_shape=jax.ShapeDtypeStruct((ntm * MOE_TILE, d), F32),
        grid_spec=pltpu.PrefetchScalarGridSpec(
            num_scalar_prefetch=3,
            grid=(s // tm,),
            in_specs=[pl.BlockSpec((tm, d), lambda i, *_: (i, 0))],
            out_specs=pl.BlockSpec(memory_space=pl.ANY),
            scratch_shapes=[pltpu.VMEM((MOE_TILE, d), F32), pltpu.SemaphoreType.DMA, pltpu.SemaphoreType.DMA]),
        compiler_params=pltpu.CompilerParams(dimension_semantics=("arbitrary",),
                                             vmem_limit_bytes=VMEM_LIMIT),
        name="dispatch",
    )(pos_flat, texp, nact, h2)


def _experts_kernel(texp_ref, nact_ref, xs_ref, wg_ref, wu_ref, wd_ref, ys_ref, wg_bf, wu_bf, wd_bf):
    i = pl.program_id(0)
    nact = nact_ref[0]
    j = jnp.minimum(i, nact - 1)
    fresh = (i == 0) | (texp_ref[j] != texp_ref[jnp.maximum(j - 1, 0)])

    @pl.when((i < nact) & fresh)
    def _():
        wg_bf[...] = wg_ref[0].astype(BF16)
        wu_bf[...] = wu_ref[0].astype(BF16)
        wd_bf[...] = wd_ref[0].astype(BF16)

    @pl.when(i < nact)
    def _():
        x = xs_ref[...].astype(BF16)
        hg = jnp.dot(x, wg_bf[...], preferred_element_type=F32)
        hu = jnp.dot(x, wu_bf[...], preferred_element_type=F32)
        a = (hg * _sigmoid(hg)) * hu
        ys_ref[...] = jnp.dot(a.astype(BF16), wd_bf[...], preferred_element_type=F32)

    @pl.when(i >= nact)
    def _():
        ys_ref[...] = jnp.zeros_like(ys_ref)


def _experts(texp, nact, xs, w_gate, w_up, w_down):
    p, d = xs.shape
    ntm = p // MOE_TILE
    _, _, f = w_gate.shape

    def tile(i, texp_ref, nact_ref):
        return (jnp.minimum(i, nact_ref[0] - 1), 0)

    def wsel(i, texp_ref, nact_ref):
        return (texp_ref[jnp.minimum(i, nact_ref[0] - 1)], 0, 0)

    return pl.pallas_call(
        _experts_kernel,
        out_shape=jax.ShapeDtypeStruct((p, d), F32),
        grid_spec=pltpu.PrefetchScalarGridSpec(
            num_scalar_prefetch=2,
            grid=(ntm,),
            in_specs=[pl.BlockSpec((MOE_TILE, d), tile),
                      pl.BlockSpec((1, d, f), wsel),
                      pl.BlockSpec((1, d, f), wsel),
                      pl.BlockSpec((1, f, d), wsel)],
            out_specs=pl.BlockSpec((MOE_TILE, d), lambda i, *_: (i, 0)),
            scratch_shapes=[pltpu.VMEM((d, f), BF16), pltpu.VMEM((d, f), BF16), pltpu.VMEM((f, d), BF16)]),
        compiler_params=pltpu.CompilerParams(dimension_semantics=("arbitrary",),
                                             vmem_limit_bytes=VMEM_LIMIT),
        name="experts",
    )(texp, nact, xs, w_gate, w_up, w_down)


def _combine_kernel(pos_ref, ys_ref, rinfo_ref, x1_ref, gpost_ref, gt_ref, o_ref, ybuf, sem, *, tm):
    i = pl.program_id(0)

    def gather(tile, b):
        def start(r, carry):
            tok = tile * tm + r
            for slot in range(2):
                p = pos_ref[2 * tok + slot]
                pltpu.make_async_copy(ys_ref.at[pl.ds(p, 1), :], ybuf.at[b, slot, pl.ds(r, 1), :],
                                      sem.at[b]).start()
            return carry
        lax.fori_loop(0, tm, start, 0, unroll=8)

    @pl.when(i == 0)
    def _():
        gather(0, 0)

    @pl.when(i + 1 < pl.num_programs(0))
    def _():
        gather(i + 1, (i + 1) % 2)

    b = i % 2
    for slot in range(2):
        pltpu.make_async_copy(ys_ref.at[pl.ds(0, tm), :], ybuf.at[b, slot], sem.at[b]).wait()
    r = rinfo_ref[...]
    y = r[:, 2:3] * ybuf[b, 0] + r[:, 3:4] * ybuf[b, 1]
    o_ref[...] = x1_ref[...] + gt_ref[...] * (_rms(y, NORM_EPS) * gpost_ref[...])


def _combine(pos_flat, ys, rinfo, x1, gpost, gt, tm=256):
    s, d = x1.shape
    row = pl.BlockSpec((1, d), lambda i, *_: (0, 0))
    return pl.pallas_call(
        functools.partial(_combine_kernel, tm=tm),
        out_shape=jax.ShapeDtypeStruct((s, d), F32),
        grid_spec=pltpu.PrefetchScalarGridSpec(
            num_scalar_prefetch=1,
            grid=(s // tm,),
            in_specs=[pl.BlockSpec(memory_space=pl.ANY),
                      pl.BlockSpec((tm, LANES), lambda i, *_: (i, 0)),
                      pl.BlockSpec((tm, d), lambda i, *_: (i, 0)),
                      row, row],
            out_specs=pl.BlockSpec((tm, d), lambda i, *_: (i, 0)),
            scratch_shapes=[pltpu.VMEM((2, 2, tm, d), F32), pltpu.SemaphoreType.DMA((2,))]),
        compiler_params=pltpu.CompilerParams(dimension_semantics=("arbitrary",),
                                             vmem_limit_bytes=VMEM_LIMIT),
        name="combine",
    )(pos_flat, ys, rinfo, x1, gpost, gt)


def _moe(h2, rinfo, w_gate, w_up, w_down, x1, gpost, gt):
    pos, tinfo = _plan(rinfo)
    pos_flat = pos[:, 0:2].astype(jnp.int32).reshape(-1)
    texp = tinfo[:, 0].astype(jnp.int32)
    nact = tinfo[0:1, 1].astype(jnp.int32)
    xs = _dispatch(pos_flat, texp, nact, h2)
    ys = _experts(texp, nact, xs, w_gate, w_up, w_down)
    return _combine(pos_flat, ys, rinfo, x1, gpost, gt)


def kernel(x, c, w_ada, b_ada, g_pre1, w_in, lam_q1, lam_k1, lam_q2, lam_k2, g_subln, w_conv, w_out,
           g_post1, g_pre2, w_router_g, b_router_g, w_router_e, b_router_e, w_gate, w_up, w_down, g_post2):
    b, s, d = x.shape
    assert b == 1
    depth = w_ada.shape[0]
    slopes = jnp.asarray([2.0 ** (-8.0 * (i + 1) / N_HEADS) for i in range(N_HEADS)], F32)
    x2 = x.reshape(s, d)
    for l in range(depth):
        lam_init = 0.8 - 0.6 * math.exp(-0.3 * l)
        mod = _ada(c.reshape(d, 1), w_ada[l], b_ada[l].reshape(1, -1))
        sh1, sc1, gt1, sh2, sc2, gt2 = [mod[:, k * d:(k + 1) * d] for k in range(6)]
        proj = _inproj(x2, g_pre1[l].reshape(1, d), sc1, sh1, w_in[l].astype(BF16))
        y_attn = _attn(slopes, lam_q1[l].reshape(1, -1), lam_k1[l].reshape(1, -1),
                       lam_q2[l].reshape(1, -1), lam_k2[l].reshape(1, -1), proj,
                       g_subln[l].reshape(1, -1), lam_init)
        wr = jnp.concatenate([w_router_g[l], w_router_e[l].reshape(d, N_EXPERTS),
                              jnp.zeros((d, LANES - N_GROUPS - N_EXPERTS), F32)], axis=1).astype(BF16)
        br = jnp.concatenate([b_router_g[l], b_router_e[l].reshape(N_EXPERTS),
                              jnp.zeros((LANES - N_GROUPS - N_EXPERTS,), F32)]).reshape(1, LANES)
        x1, h2, rinfo = _outproj(y_attn, proj, x2, w_conv[l], w_out[l].astype(BF16),
                                 g_post1[l].reshape(1, d), gt1, g_pre2[l].reshape(1, d), sc2, sh2, wr, br)
        x2 = _moe(h2, rinfo, w_gate[l], w_up[l], w_down[l], x1, g_post2[l].reshape(1, d), gt2)
    return x2.reshape(b, s, d)
```

```python
import functools
import math

import jax
import jax.numpy as jnp
from jax import lax
from jax.experimental import pallas as pl
from jax.experimental.pallas import tpu as pltpu

F32 = jnp.float32
BF16 = jnp.bfloat16

LANES = 128
CHUNK = 64
N_HEADS = 8
QK_DIM = 64
V_DIM = 128
CONV_WIDTH = 3
N_GROUPS = 4
EXPERTS_PER_GROUP = 8
N_EXPERTS = N_GROUPS * EXPERTS_PER_GROUP
NORM_EPS = 1e-6
SUBLN_EPS = 1e-5
NEG_BIG = -1e30
LOG2E = 1.4426950408889634
VMEM_LIMIT = 56 * 1024 * 1024


def _rms(x, eps):
    return x * lax.rsqrt(jnp.mean(x * x, axis=-1, keepdims=True) + eps)


def _sigmoid(x):
    return 1.0 / (1.0 + jnp.exp(-x))


def _ada_kernel(c_ref, w_ref, b_ref, o_ref):
    c = c_ref[...]
    s = c * _sigmoid(c)
    o_ref[...] = jnp.sum(w_ref[...] * s, axis=0, keepdims=True) + b_ref[...]


def _ada(c_col, w_ada, b_ada, tn=1024):
    d, n = w_ada.shape
    return pl.pallas_call(
        _ada_kernel,
        out_shape=jax.ShapeDtypeStruct((1, n), F32),
        grid=(n // tn,),
        in_specs=[pl.BlockSpec((d, 1), lambda j: (0, 0)),
                  pl.BlockSpec((d, tn), lambda j: (0, j)),
                  pl.BlockSpec((1, tn), lambda j: (0, j))],
        out_specs=pl.BlockSpec((1, tn), lambda j: (0, j)),
        compiler_params=pltpu.CompilerParams(dimension_semantics=("arbitrary",),
                                             vmem_limit_bytes=VMEM_LIMIT),
        name="ada",
    )(c_col, w_ada, b_ada)


def _inproj_kernel(x_ref, g_ref, sc_ref, sh_ref, w_ref, o_ref, h_scr, *, cn):
    @pl.when(pl.program_id(1) == 0)
    def _():
        y = _rms(x_ref[...], NORM_EPS) * g_ref[...]
        h_scr[...] = (y * (1.0 + sc_ref[...]) + sh_ref[...]).astype(BF16)

    acc = jnp.dot(h_scr[...], w_ref[...].astype(BF16), preferred_element_type=F32)
    for c in range(cn):
        o_ref[c] = acc[:, c * LANES:(c + 1) * LANES].astype(BF16)


def _inproj(x2, g, sc, sh, w, tm=1024, cn=4):
    s, d = x2.shape
    n = w.shape[1]
    nchunks = n // LANES
    row = lambda i, j: (0, 0)
    return pl.pallas_call(
        functools.partial(_inproj_kernel, cn=cn),
        out_shape=jax.ShapeDtypeStruct((nchunks, s, LANES), BF16),
        grid=(s // tm, nchunks // cn),
        in_specs=[pl.BlockSpec((tm, d), lambda i, j: (i, 0)),
                  pl.BlockSpec((1, d), row),
                  pl.BlockSpec((1, d), row),
                  pl.BlockSpec((1, d), row),
                  pl.BlockSpec((d, cn * LANES), lambda i, j: (0, j))],
        out_specs=pl.BlockSpec((cn, tm, LANES), lambda i, j: (j, i, 0)),
        scratch_shapes=[pltpu.VMEM((tm, d), BF16)],
        compiler_params=pltpu.CompilerParams(dimension_semantics=("arbitrary", "arbitrary"),
                                             vmem_limit_bytes=VMEM_LIMIT),
        name="inproj",
    )(x2, g, sc, sh, w)


ATT_POS = 256
ATT_VROWS = 144


def _attn_kernel(slope_ref, lq1_ref, lk1_ref, lq2_ref, lk2_ref, q_ref, k_ref, v_ref, gsub_ref,
                 o_ref, vt_scr, sa_scr, sb_scr, m_scr, acc_scr, corr_scr, *, t, lam_init):
    h = pl.program_id(0)
    qi = pl.program_id(1)
    nkt = vt_scr.shape[0]
    slope2 = slope_ref[h] * LOG2E
    c_hi = jnp.full((LANES, t), slope2, F32).astype(BF16).astype(F32)
    c_lo = (jnp.full((LANES, t), slope2, F32) - c_hi).astype(BF16).astype(F32)

    @pl.when(qi == 0)
    def _():
        ones_row = jnp.where(lax.broadcasted_iota(jnp.int32, (ATT_VROWS - V_DIM, t), 0) == 0, 1.0, 0.0)

        def tr(c, carry):
            vc = v_ref[0, pl.ds(pl.multiple_of(c * t, t), t), :]
            vt_scr[c] = jnp.concatenate([vc.astype(F32).T, ones_row], axis=0).astype(BF16)
            return carry
        lax.fori_loop(0, nkt, tr, 0)

        krel = lax.broadcasted_iota(jnp.int32, (t, t), 0)
        qrel = lax.broadcasted_iota(jnp.int32, (t, t), 1)
        shift = CHUNK.bit_length() - 1
        allowed = (krel >> shift) <= (qrel >> shift)
        c_mxu = (c_hi + c_lo)[0:1, 0:1]
        corr_scr[...] = jnp.where(allowed, slope2 * (qrel - jnp.abs(qrel - krel)).astype(F32)
                                  - c_mxu * krel.astype(F32), NEG_BIG)

    step_dec = slope2 * t
    qt = (q_ref[0].astype(F32) * (LOG2E * QK_DIM ** -0.5)).T
    row = lax.broadcasted_iota(jnp.int32, (LANES, t), 0)
    qfeat = jnp.where(row == 0, c_hi, jnp.where(row == 1, ATT_POS * c_hi,
                      jnp.where(row == 2, c_lo, jnp.where(row == 3, ATT_POS * c_lo, 0.0))))
    qa = jnp.concatenate([jnp.where(row < QK_DIM, qt, 0.0), qfeat], axis=0)
    qb = jnp.concatenate([jnp.where(row >= QK_DIM, qt, 0.0), qfeat], axis=0)
    qts = jnp.concatenate([qa, qb], axis=1).astype(BF16)

    krow = lax.broadcasted_iota(jnp.int32, (t, LANES), 0)
    klane = lax.broadcasted_iota(jnp.int32, (t, LANES), 1)
    pbits = ATT_POS.bit_length() - 1
    kmod = (krow & (ATT_POS - 1)).astype(F32)
    kdiv = (krow >> pbits).astype(F32)
    kfeat = jnp.where((klane == 0) | (klane == 2), kmod,
                      jnp.where((klane == 1) | (klane == 3), kdiv, 0.0)).astype(BF16)

    m_scr[...] = jnp.full(m_scr.shape, NEG_BIG, F32)
    acc_scr[...] = jnp.zeros(acc_scr.shape, F32)

    def scores(j, s_ref):
        kj = k_ref[0, pl.ds(pl.multiple_of(j * t, t), t), :]
        s_ref[...] = jnp.dot(jnp.concatenate([kj, kfeat], axis=1), qts, preferred_element_type=F32)

    def colmax(s):
        while s.shape[0] > 8:
            half = s.shape[0] // 2
            s = jnp.maximum(s[:half], s[half:])
        return jnp.max(s, axis=0, keepdims=True)

    def softmax_pv(j, s_ref, corr):
        s = s_ref[...]
        if corr is not None:
            s = s + corr
        m_old = m_scr[...] - step_dec
        m_new = jnp.maximum(m_old, colmax(s))
        alpha = jnp.exp2(m_old - m_new)
        p = jnp.exp2(s - m_new).astype(BF16)
        pv = jnp.dot(vt_scr[j], p, preferred_element_type=F32)
        acc_scr[...] = alpha * acc_scr[...] + pv
        m_scr[...] = m_new

    def near(s_ref):
        corr = corr_scr[...]
        softmax_pv(qi, s_ref, jnp.concatenate([corr, corr], axis=1))

    scores(0, sa_scr)

    def pair(i, carry):
        scores(2 * i + 1, sb_scr)
        softmax_pv(2 * i, sa_scr, None)
        scores(2 * i + 2, sa_scr)
        softmax_pv(2 * i + 1, sb_scr, None)
        return carry

    def quad(i, carry):
        return pair(2 * i + 1, pair(2 * i, carry))

    def octo(i, carry):
        return quad(2 * i + 1, quad(2 * i, carry))

    lax.fori_loop(0, qi // 8, octo, 0)
    lax.fori_loop(2 * (qi // 8), qi // 4, quad, 0)
    lax.fori_loop(2 * (qi // 4), qi // 2, pair, 0)

    @pl.when(qi % 2 == 1)
    def _():
        scores(qi, sb_scr)
        softmax_pv(qi - 1, sa_scr, None)
        near(sb_scr)

    @pl.when(qi % 2 == 0)
    def _():
        near(sa_scr)

    lam = (jnp.exp(jnp.sum(lq1_ref[...] * lk1_ref[...], axis=-1, keepdims=True))
           - jnp.exp(jnp.sum(lq2_ref[...] * lk2_ref[...], axis=-1, keepdims=True)) + lam_init)
    o1 = acc_scr[0:V_DIM, 0:t] / acc_scr[V_DIM:V_DIM + 1, 0:t]
    o2 = acc_scr[0:V_DIM, t:2 * t] / acc_scr[V_DIM:V_DIM + 1, t:2 * t]
    ot = o1 - lam * o2
    ot = ot * lax.rsqrt(jnp.mean(ot * ot, axis=0, keepdims=True) + SUBLN_EPS)
    o_ref[...] = (ot.T * gsub_ref[...] * (1.0 - lam_init)).astype(BF16)


def _attn(slopes, lq1, lk1, lq2, lk2, proj, gsub, lam_init, t=512):
    s = proj.shape[1]
    smem = pl.BlockSpec(memory_space=pltpu.SMEM)
    vec = lambda n: pl.BlockSpec((1, n), lambda h, i: (0, 0))
    return pl.pallas_call(
        functools.partial(_attn_kernel, t=t, lam_init=lam_init),
        out_shape=jax.ShapeDtypeStruct((s, N_HEADS * V_DIM), BF16),
        grid=(N_HEADS, s // t),
        in_specs=[smem, vec(QK_DIM), vec(QK_DIM), vec(QK_DIM), vec(QK_DIM),
                  pl.BlockSpec((1, t, LANES), lambda h, i: (h, i, 0)),
                  pl.BlockSpec((1, s, LANES), lambda h, i: (N_HEADS + h, 0, 0)),
                  pl.BlockSpec((1, s, LANES), lambda h, i: (2 * N_HEADS + h, 0, 0)),
                  vec(V_DIM)],
        out_specs=pl.BlockSpec((t, V_DIM), lambda h, i: (i, h)),
        scratch_shapes=[pltpu.VMEM((s // t, ATT_VROWS, t), BF16),
                        pltpu.VMEM((t, 2 * t), F32), pltpu.VMEM((t, 2 * t), F32),
                        pltpu.VMEM((1, 2 * t), F32),
                        pltpu.VMEM((ATT_VROWS, 2 * t), F32),
                        pltpu.VMEM((t, t), F32)],
        compiler_params=pltpu.CompilerParams(dimension_semantics=("arbitrary", "arbitrary"),
                                             vmem_limit_bytes=VMEM_LIMIT),
        name="attn",
    )(slopes, lq1, lk1, lq2, lk2, proj, proj, proj, gsub)


def _route(logits):
    lane = lax.broadcasted_iota(jnp.int32, logits.shape, 1).astype(F32)
    far = float(4 * LANES)
    gl = jnp.where(lane < N_GROUPS, logits, NEG_BIG)
    gmax = jnp.max(gl, axis=-1, keepdims=True)
    g_w = 1.0 / jnp.sum(jnp.exp(gl - gmax), axis=-1, keepdims=True)
    g_idx = jnp.min(jnp.where(gl == gmax, lane, far), axis=-1, keepdims=True)
    lo = N_GROUPS + EXPERTS_PER_GROUP * g_idx
    el = jnp.where((lane >= lo) & (lane < lo + EXPERTS_PER_GROUP), logits, NEG_BIG)
    e1 = jnp.max(el, axis=-1, keepdims=True)
    i1 = jnp.min(jnp.where(el == e1, lane, far), axis=-1, keepdims=True)
    el2 = jnp.where(lane == i1, NEG_BIG, el)
    e2 = jnp.max(el2, axis=-1, keepdims=True)
    i2 = jnp.min(jnp.where(el2 == e2, lane, far), axis=-1, keepdims=True)
    r = jnp.exp(e2 - e1)
    w1 = 1.0 / (1.0 + r)
    w2 = r * w1
    out = jnp.where(lane == 0.0, i1 - N_GROUPS, 0.0)
    out = jnp.where(lane == 1.0, i2 - N_GROUPS, out)
    out = jnp.where(lane == 2.0, g_w * w1, out)
    out = jnp.where(lane == 3.0, g_w * w2, out)
    return out


def _outproj_kernel(ya_ref, b_ref, c_ref, u_ref, x_ref, wconv_ref, wout_ref, gpost_ref, gt_ref,
                    gpre_ref, sc_ref, sh_ref, wr_ref, br_ref,
                    x1_ref, h2_ref, rinfo_ref, zbuf, ybuf, *, tm, nc):
    i = pl.program_id(0)
    halo = 8

    @pl.when(i == 0)
    def _():
        zbuf[0:halo, :] = jnp.zeros((halo, zbuf.shape[1]), F32)

    @pl.when(i > 0)
    def _():
        zbuf[0:halo, :] = zbuf[tm:tm + halo, :]

    for c in range(nc):
        cols = slice(c * LANES, (c + 1) * LANES)
        zbuf[halo:halo + tm, cols] = c_ref[c].astype(F32) * u_ref[c].astype(F32)
    d_attn = ya_ref.shape[1]
    ybuf[:, 0:d_attn] = ya_ref[...]
    for c in range(nc):
        cols = slice(c * LANES, (c + 1) * LANES)
        conv = (wconv_ref[0:1, cols] * zbuf[halo - 2:halo - 2 + tm, cols]
                + wconv_ref[1:2, cols] * zbuf[halo - 1:halo - 1 + tm, cols]
                + wconv_ref[2:3, cols] * zbuf[halo:halo + tm, cols])
        ybuf[:, d_attn + c * LANES:d_attn + (c + 1) * LANES] = (b_ref[c].astype(F32) * conv).astype(BF16)

    y = jnp.dot(ybuf[...], wout_ref[...], preferred_element_type=F32)
    x1 = x_ref[...] + gt_ref[...] * (_rms(y, NORM_EPS) * gpost_ref[...])
    x1_ref[...] = x1
    h2 = (_rms(x1, NORM_EPS) * gpre_ref[...]) * (1.0 + sc_ref[...]) + sh_ref[...]
    h2_ref[...] = h2
    logits = jnp.dot(h2.astype(BF16), wr_ref[...], preferred_element_type=F32) + br_ref[...]
    rinfo_ref[...] = _route(logits)


def _outproj(y_attn, proj, x2, w_conv, wout_bf, gpost, gt, gpre, sc, sh, wr_bf, br, tm=512):
    s, d = x2.shape
    d_attn = y_attn.shape[1]
    nc = (d - d_attn) // LANES
    base = 3 * d_attn // LANES
    row = lambda n: pl.BlockSpec((1, n), lambda i: (0, 0))
    blk = lambda g: pl.BlockSpec((nc, tm, LANES), lambda i, g=g: (base // nc + g, i, 0))
    return pl.pallas_call(
        functools.partial(_outproj_kernel, tm=tm, nc=nc),
        out_shape=(jax.ShapeDtypeStruct((s, d), F32),
                   jax.ShapeDtypeStruct((s, d), F32),
                   jax.ShapeDtypeStruct((s, LANES), F32)),
        grid=(s // tm,),
        in_specs=[pl.BlockSpec((tm, d_attn), lambda i: (i, 0)),
                  blk(0), blk(1), blk(2),
                  pl.BlockSpec((tm, d), lambda i: (i, 0)),
                  pl.BlockSpec((CONV_WIDTH, d - d_attn), lambda i: (0, 0)),
                  pl.BlockSpec((d, d), lambda i: (0, 0)),
                  row(d), row(d), row(d), row(d), row(d),
                  pl.BlockSpec((d, LANES), lambda i: (0, 0)),
                  row(LANES)],
        out_specs=(pl.BlockSpec((tm, d), lambda i: (i, 0)),
                   pl.BlockSpec((tm, d), lambda i: (i, 0)),
                   pl.BlockSpec((tm, LANES), lambda i: (i, 0))),
        scratch_shapes=[pltpu.VMEM((tm + 8, d - d_attn), F32), pltpu.VMEM((tm, d), BF16)],
        compiler_params=pltpu.CompilerParams(dimension_semantics=("arbitrary",),
                                             vmem_limit_bytes=VMEM_LIMIT),
        name="outproj",
    )(y_attn, proj, proj, proj, x2, w_conv, wout_bf, gpost, gt, gpre, sc, sh, wr_bf, br)


MOE_TILE = 256


def _moe_num_tiles(s):
    return (2 * s) // MOE_TILE + N_EXPERTS


def _plan_kernel(rinfo_ref, pos_ref, tinfo_ref, cnt_scr, base_scr, *, tm, ntm):
    ph = pl.program_id(0)
    i = pl.program_id(1)
    r = rinfo_ref[...]
    lane = lax.broadcasted_iota(jnp.int32, r.shape, 1).astype(F32)
    oh0 = jnp.where(lane == r[:, 0:1], 1.0, 0.0)
    oh1 = jnp.where(lane == r[:, 1:2], 1.0, 0.0)
    both = oh0 + oh1
    tile_cnt = jnp.sum(both, axis=0, keepdims=True)

    @pl.when((ph == 0) & (i == 0))
    def _():
        cnt_scr[...] = jnp.zeros_like(cnt_scr)

    @pl.when(ph == 0)
    def _():
        cnt_scr[...] += tile_cnt

    @pl.when((ph == 1) & (i == 0))
    def _():
        ktiles = jnp.floor((cnt_scr[...] + (MOE_TILE - 1)) * (1.0 / MOE_TILE))
        rr = lax.broadcasted_iota(jnp.int32, (LANES, LANES), 0)
        cc = lax.broadcasted_iota(jnp.int32, (LANES, LANES), 1)
        upper = jnp.where(rr < cc, 1.0, 0.0).astype(BF16)
        first = jnp.dot(jnp.broadcast_to(ktiles, (8, LANES)).astype(BF16), upper,
                        preferred_element_type=F32)[0:1, :]
        base_scr[...] = first * MOE_TILE
        ti = lax.broadcasted_iota(jnp.int32, (ntm, LANES), 0).astype(F32)
        el = lax.broadcasted_iota(jnp.int32, (ntm, LANES), 1).astype(F32)
        owned = jnp.where(ti >= first, jnp.where(ti < first + ktiles, 1.0, 0.0), 0.0)
        texp = jnp.sum(owned * el, axis=1, keepdims=True)
        nact = jnp.sum(ktiles, axis=1, keepdims=True)
        tinfo_ref[...] = jnp.where(el == 0.0, texp, jnp.where(el == 1.0, nact, 0.0))

    @pl.when(ph == 1)
    def _():
        rr = lax.broadcasted_iota(jnp.int32, (tm, tm), 0)
        cc = lax.broadcasted_iota(jnp.int32, (tm, tm), 1)
        lower = jnp.where(cc < rr, 1.0, 0.0).astype(BF16)
        before = jnp.dot(lower, both.astype(BF16), preferred_element_type=F32)
        val = before + base_scr[...]
        p0 = jnp.sum(oh0 * val, axis=1, keepdims=True)
        p1 = jnp.sum(oh1 * val, axis=1, keepdims=True)
        pos_ref[...] = jnp.where(lane == 0.0, p0, jnp.where(lane == 1.0, p1, 0.0))
        base_scr[...] += tile_cnt


def _plan(rinfo, tm=512):
    s = rinfo.shape[0]
    ntm = _moe_num_tiles(s)
    return pl.pallas_call(
        functools.partial(_plan_kernel, tm=tm, ntm=ntm),
        out_shape=(jax.ShapeDtypeStruct((s, LANES), F32), jax.ShapeDtypeStruct((ntm, LANES), F32)),
        grid=(2, s // tm),
        in_specs=[pl.BlockSpec((tm, LANES), lambda p, i: (i, 0))],
        out_specs=(pl.BlockSpec((tm, LANES), lambda p, i: (i * p, 0)),
                   pl.BlockSpec((ntm, LANES), lambda p, i: (0, 0))),
        scratch_shapes=[pltpu.VMEM((1, LANES), F32), pltpu.VMEM((1, LANES), F32)],
        compiler_params=pltpu.CompilerParams(dimension_semantics=("arbitrary", "arbitrary"),
                                             vmem_limit_bytes=VMEM_LIMIT),
        name="plan",
    )(rinfo)


def _dispatch_kernel(pos_ref, texp_ref, nact_ref, h2_ref, xs_ref, zero_scr, sem, zsem, *, tm):
    i = pl.program_id(0)

    @pl.when(i == 0)
    def _():
        zero_scr[...] = jnp.zeros_like(zero_scr)
        nact = nact_ref[0]
        ntm = xs_ref.shape[0] // MOE_TILE

        def is_last(t):
            nxt = texp_ref[jnp.minimum(t + 1, nact - 1)]
            return (t >= nact - 1) | (nxt != texp_ref[jnp.minimum(t, nact - 1)])

        def zstart(t, carry):
            @pl.when(is_last(t))
            def _():
                pltpu.make_async_copy(zero_scr, xs_ref.at[pl.ds(pl.multiple_of(t * MOE_TILE, MOE_TILE), MOE_TILE), :],
                                      zsem).start()
            return carry

        def zwait(t, carry):
            @pl.when(is_last(t))
            def _():
                pltpu.make_async_copy(zero_scr, xs_ref.at[pl.ds(0, MOE_TILE), :], zsem).wait()
            return carry

        lax.fori_loop(0, ntm, zstart, 0)
        lax.fori_loop(0, ntm, zwait, 0)

    def start(r, carry):
        tok = i * tm + r
        for slot in range(2):
            p = pos_ref[2 * tok + slot]
            pltpu.make_async_copy(h2_ref.at[pl.ds(r, 1), :], xs_ref.at[pl.ds(p, 1), :], sem).start()
        return carry

    lax.fori_loop(0, tm, start, 0, unroll=8)
    for slot in range(2):
        pltpu.make_async_copy(h2_ref, xs_ref.at[pl.ds(0, tm), :], sem).wait()


def _dispatch(pos_flat, texp, nact, h2, tm=256):
    s, d = h2.shape
    ntm = _moe_num_tiles(s)
    return pl.pallas_call(
        functools.partial(_dispatch_kernel, tm=tm),
        out_shape=jax.ShapeDtypeStruct((ntm * MOE_TILE, d), F32),
        grid_spec=pltpu.PrefetchScalarGridSpec(
            num_scalar_prefetch=3,
            grid=(s // tm,),
            in_specs=[pl.BlockSpec((tm, d), lambda i, *_: (i, 0))],
            out_specs=pl.BlockSpec(memory_space=pl.ANY),
            scratch_shapes=[pltpu.VMEM((MOE_TILE, d), F32), pltpu.SemaphoreType.DMA, pltpu.SemaphoreType.DMA]),
        compiler_params=pltpu.CompilerParams(dimension_semantics=("arbitrary",),
                                             vmem_limit_bytes=VMEM_LIMIT),
        name="dispatch",
    )(pos_flat, texp, nact, h2)


def _experts_kernel(texp_ref, nact_ref, xs_ref, wg_hbm, wu_hbm, wd_hbm, ys_ref,
                    wg_f32, wu_f32, wd_f32, wg_bf, wu_bf, wd_bf, ord_scr, sem):
    i = pl.program_id(0)
    nact = nact_ref[0]
    j = jnp.minimum(i, nact - 1)
    e = texp_ref[j]
    fresh = (i == 0) | (e != texp_ref[jnp.maximum(j - 1, 0)])

    def weight_copies(expert, slot):
        return (pltpu.make_async_copy(wg_hbm.at[expert], wg_f32.at[slot], sem.at[slot]),
                pltpu.make_async_copy(wu_hbm.at[expert], wu_f32.at[slot], sem.at[slot]),
                pltpu.make_async_copy(wd_hbm.at[expert], wd_f32.at[slot], sem.at[slot]))

    @pl.when(i == 0)
    def _():
        ord_scr[0] = 0
        for c in weight_copies(e, 0):
            c.start()

    @pl.when((i < nact) & fresh)
    def _():
        slot = ord_scr[0] % 2
        for c in weight_copies(e, slot):
            c.wait()
        wg_bf[...] = wg_f32[slot].astype(BF16)
        wu_bf[...] = wu_f32[slot].astype(BF16)
        wd_bf[...] = wd_f32[slot].astype(BF16)
        nxt = lax.while_loop(lambda t: (t < nact) & (texp_ref[jnp.minimum(t, nact - 1)] == e),
                             lambda t: t + 1, i + 1)

        @pl.when(nxt < nact)
        def _():
            for c in weight_copies(texp_ref[nxt], 1 - slot):
                c.start()
        ord_scr[0] = ord_scr[0] + 1

    @pl.when(i < nact)
    def _():
        x = xs_ref[...].astype(BF16)
        hg = jnp.dot(x, wg_bf[...], preferred_element_type=F32)
        hu = jnp.dot(x, wu_bf[...], preferred_element_type=F32)
        a = (hg * _sigmoid(hg)) * hu
        ys_ref[...] = jnp.dot(a.astype(BF16), wd_bf[...], preferred_element_type=F32)

    @pl.when(i >= nact)
    def _():
        ys_ref[...] = jnp.zeros_like(ys_ref)


def _experts(texp, nact, xs, w_gate, w_up, w_down):
    p, d = xs.shape
    ntm = p // MOE_TILE
    _, _, f = w_gate.shape

    def tile(i, texp_ref, nact_ref):
        return (jnp.minimum(i, nact_ref[0] - 1), 0)

    hbm = pl.BlockSpec(memory_space=pl.ANY)
    return pl.pallas_call(
        _experts_kernel,
        out_shape=jax.ShapeDtypeStruct((p, d), F32),
        grid_spec=pltpu.PrefetchScalarGridSpec(
            num_scalar_prefetch=2,
            grid=(ntm,),
            in_specs=[pl.BlockSpec((MOE_TILE, d), tile), hbm, hbm, hbm],
            out_specs=pl.BlockSpec((MOE_TILE, d), lambda i, *_: (i, 0)),
            scratch_shapes=[pltpu.VMEM((2, d, f), F32), pltpu.VMEM((2, d, f), F32), pltpu.VMEM((2, f, d), F32),
                            pltpu.VMEM((d, f), BF16), pltpu.VMEM((d, f), BF16), pltpu.VMEM((f, d), BF16),
                            pltpu.SMEM((1,), jnp.int32), pltpu.SemaphoreType.DMA((2,))]),
        compiler_params=pltpu.CompilerParams(dimension_semantics=("arbitrary",),
                                             vmem_limit_bytes=VMEM_LIMIT),
        name="experts",
    )(texp, nact, xs, w_gate, w_up, w_down)


def _combine_kernel(pos_ref, ys_ref, rinfo_ref, x1_ref, gpost_ref, gt_ref, o_ref, ybuf, sem, *, tm):
    i = pl.program_id(0)

    def gather(tile, b):
        def start(r, carry):
            tok = tile * tm + r
            for slot in range(2):
                p = pos_ref[2 * tok + slot]
                pltpu.make_async_copy(ys_ref.at[pl.ds(p, 1), :], ybuf.at[b, slot, pl.ds(r, 1), :],
                                      sem.at[b]).start()
            return carry
        lax.fori_loop(0, tm, start, 0, unroll=8)

    @pl.when(i == 0)
    def _():
        gather(0, 0)

    @pl.when(i + 1 < pl.num_programs(0))
    def _():
        gather(i + 1, (i + 1) % 2)

    b = i % 2
    for slot in range(2):
        pltpu.make_async_copy(ys_ref.at[pl.ds(0, tm), :], ybuf.at[b, slot], sem.at[b]).wait()
    r = rinfo_ref[...]
    y = r[:, 2:3] * ybuf[b, 0] + r[:, 3:4] * ybuf[b, 1]
    o_ref[...] = x1_ref[...] + gt_ref[...] * (_rms(y, NORM_EPS) * gpost_ref[...])


def _combine(pos_flat, ys, rinfo, x1, gpost, gt, tm=256):
    s, d = x1.shape
    row = pl.BlockSpec((1, d), lambda i, *_: (0, 0))
    return pl.pallas_call(
        functools.partial(_combine_kernel, tm=tm),
        out_shape=jax.ShapeDtypeStruct((s, d), F32),
        grid_spec=pltpu.PrefetchScalarGridSpec(
            num_scalar_prefetch=1,
            grid=(s // tm,),
            in_specs=[pl.BlockSpec(memory_space=pl.ANY),
                      pl.BlockSpec((tm, LANES), lambda i, *_: (i, 0)),
                      pl.BlockSpec((tm, d), lambda i, *_: (i, 0)),
                      row, row],
            out_specs=pl.BlockSpec((tm, d), lambda i, *_: (i, 0)),
            scratch_shapes=[pltpu.VMEM((2, 2, tm, d), F32), pltpu.SemaphoreType.DMA((2,))]),
        compiler_params=pltpu.CompilerParams(dimension_semantics=("arbitrary",),
                                             vmem_limit_bytes=VMEM_LIMIT),
        name="combine",
    )(pos_flat, ys, rinfo, x1, gpost, gt)


def _moe(h2, rinfo, w_gate, w_up, w_down, x1, gpost, gt):
    pos, tinfo = _plan(rinfo)
    pos_flat = pos[:, 0:2].astype(jnp.int32).reshape(-1)
    texp = tinfo[:, 0].astype(jnp.int32)
    nact = tinfo[0:1, 1].astype(jnp.int32)
    xs = _dispatch(pos_flat, texp, nact, h2)
    ys = _experts(texp, nact, xs, w_gate, w_up, w_down)
    return _combine(pos_flat, ys, rinfo, x1, gpost, gt)


def kernel(x, c, w_ada, b_ada, g_pre1, w_in, lam_q1, lam_k1, lam_q2, lam_k2, g_subln, w_conv, w_out,
           g_post1, g_pre2, w_router_g, b_router_g, w_router_e, b_router_e, w_gate, w_up, w_down, g_post2):
    b, s, d = x.shape
    assert b == 1
    depth = w_ada.shape[0]
    slopes = jnp.asarray([2.0 ** (-8.0 * (i + 1) / N_HEADS) for i in range(N_HEADS)], F32)
    x2 = x.reshape(s, d)
    for l in range(depth):
        lam_init = 0.8 - 0.6 * math.exp(-0.3 * l)
        mod = _ada(c.reshape(d, 1), w_ada[l], b_ada[l].reshape(1, -1))
        sh1, sc1, gt1, sh2, sc2, gt2 = [mod[:, k * d:(k + 1) * d] for k in range(6)]
        proj = _inproj(x2, g_pre1[l].reshape(1, d), sc1, sh1, w_in[l])
        y_attn = _attn(slopes, lam_q1[l].reshape(1, -1), lam_k1[l].reshape(1, -1),
                       lam_q2[l].reshape(1, -1), lam_k2[l].reshape(1, -1), proj,
                       g_subln[l].reshape(1, -1), lam_init)
        wr = jnp.concatenate([w_router_g[l], w_router_e[l].reshape(d, N_EXPERTS),
                              jnp.zeros((d, LANES - N_GROUPS - N_EXPERTS), F32)], axis=1).astype(BF16)
        br = jnp.concatenate([b_router_g[l], b_router_e[l].reshape(N_EXPERTS),
                              jnp.zeros((LANES - N_GROUPS - N_EXPERTS,), F32)]).reshape(1, LANES)
        x1, h2, rinfo = _outproj(y_attn, proj, x2, w_conv[l], w_out[l].astype(BF16),
                                 g_post1[l].reshape(1, d), gt1, g_pre2[l].reshape(1, d), sc2, sh2, wr, br)
        x2 = _moe(h2, rinfo, w_gate[l], w_up[l], w_down[l], x1, g_post2[l].reshape(1, d), gt2)
    return x2.reshape(b, s, d)
```

```python
import functools
import math

import jax
import jax.numpy as jnp
from jax import lax
from jax.experimental import pallas as pl
from jax.experimental.pallas import tpu as pltpu

F32 = jnp.float32
BF16 = jnp.bfloat16

LANES = 128
CHUNK = 64
N_HEADS = 8
QK_DIM = 64
V_DIM = 128
CONV_WIDTH = 3
N_GROUPS = 4
EXPERTS_PER_GROUP = 8
N_EXPERTS = N_GROUPS * EXPERTS_PER_GROUP
NORM_EPS = 1e-6
SUBLN_EPS = 1e-5
NEG_BIG = -1e30
LOG2E = 1.4426950408889634
VMEM_LIMIT = 56 * 1024 * 1024


def _rms(x, eps):
    return x * lax.rsqrt(jnp.mean(x * x, axis=-1, keepdims=True) + eps)


def _sigmoid(x):
    return 1.0 / (1.0 + jnp.exp(-x))


def _ada_kernel(c_ref, w_ref, b_ref, o_ref):
    c = c_ref[...]
    s = c * _sigmoid(c)
    o_ref[...] = jnp.sum(w_ref[...] * s, axis=0, keepdims=True) + b_ref[...]


def _ada(c_col, w_ada, b_ada, tn=1024):
    d, n = w_ada.shape
    return pl.pallas_call(
        _ada_kernel,
        out_shape=jax.ShapeDtypeStruct((1, n), F32),
        grid=(n // tn,),
        in_specs=[pl.BlockSpec((d, 1), lambda j: (0, 0)),
                  pl.BlockSpec((d, tn), lambda j: (0, j)),
                  pl.BlockSpec((1, tn), lambda j: (0, j))],
        out_specs=pl.BlockSpec((1, tn), lambda j: (0, j)),
        compiler_params=pltpu.CompilerParams(dimension_semantics=("arbitrary",),
                                             vmem_limit_bytes=VMEM_LIMIT),
        name="ada",
    )(c_col, w_ada, b_ada)


def _inproj_kernel(x_ref, g_ref, sc_ref, sh_ref, w_ref, o_ref, h_scr, *, cn):
    @pl.when(pl.program_id(1) == 0)
    def _():
        rows = 16

        scale = g_ref[...] * (1.0 + sc_ref[...])

        def norm_rows(r, carry):
            sl = pl.ds(pl.multiple_of(r * rows, rows), rows)
            h_scr[sl, :] = (_rms(x_ref[sl, :], NORM_EPS) * scale + sh_ref[...]).astype(BF16)
            return carry
        lax.fori_loop(0, x_ref.shape[0] // rows, norm_rows, 0, unroll=8)

    acc = jnp.dot(h_scr[...], w_ref[...].astype(BF16), preferred_element_type=F32)
    for c in range(cn):
        o_ref[c] = acc[:, c * LANES:(c + 1) * LANES].astype(BF16)


def _inproj(x2, g, sc, sh, w, tm=1024, cn=4):
    s, d = x2.shape
    n = w.shape[1]
    nchunks = n // LANES
    row = lambda i, j: (0, 0)
    return pl.pallas_call(
        functools.partial(_inproj_kernel, cn=cn),
        out_shape=jax.ShapeDtypeStruct((nchunks, s, LANES), BF16),
        grid=(s // tm, nchunks // cn),
        in_specs=[pl.BlockSpec((tm, d), lambda i, j: (i, 0)),
                  pl.BlockSpec((1, d), row),
                  pl.BlockSpec((1, d), row),
                  pl.BlockSpec((1, d), row),
                  pl.BlockSpec((d, cn * LANES), lambda i, j: (0, j))],
        out_specs=pl.BlockSpec((cn, tm, LANES), lambda i, j: (j, i, 0)),
        scratch_shapes=[pltpu.VMEM((tm, d), BF16)],
        compiler_params=pltpu.CompilerParams(dimension_semantics=("arbitrary", "arbitrary"),
                                             vmem_limit_bytes=VMEM_LIMIT),
        name="inproj",
    )(x2, g, sc, sh, w)


ATT_POS = 256
ATT_VROWS = 144


def _attn_kernel(slope_ref, lq1_ref, lk1_ref, lq2_ref, lk2_ref, q_ref, k_ref, v_ref, gsub_ref,
                 o_ref, vt_scr, sa_scr, sb_scr, m_scr, acc_scr, corr_scr, *, t, lam_init):
    h = pl.program_id(0)
    qi = pl.program_id(1)
    nkt = vt_scr.shape[0]
    slope2 = slope_ref[h] * LOG2E
    c_hi = jnp.full((LANES, t), slope2, F32).astype(BF16).astype(F32)
    c_lo = (jnp.full((LANES, t), slope2, F32) - c_hi).astype(BF16).astype(F32)

    @pl.when(qi == 0)
    def _():
        ones_row = jnp.where(lax.broadcasted_iota(jnp.int32, (ATT_VROWS - V_DIM, t), 0) == 0, 1.0, 0.0)

        def tr(c, carry):
            vc = v_ref[0, pl.ds(pl.multiple_of(c * t, t), t), :]
            vt_scr[c] = jnp.concatenate([vc.astype(F32).T, ones_row], axis=0).astype(BF16)
            return carry
        lax.fori_loop(0, nkt, tr, 0)

        krel = lax.broadcasted_iota(jnp.int32, (t, t), 0)
        qrel = lax.broadcasted_iota(jnp.int32, (t, t), 1)
        shift = CHUNK.bit_length() - 1
        allowed = (krel >> shift) <= (qrel >> shift)
        c_mxu = (c_hi + c_lo)[0:1, 0:1]
        corr_scr[...] = jnp.where(allowed, slope2 * (qrel - jnp.abs(qrel - krel)).astype(F32)
                                  - c_mxu * krel.astype(F32), NEG_BIG)

    step_dec = slope2 * t
    qt = (q_ref[0].astype(F32) * (LOG2E * QK_DIM ** -0.5)).T
    row = lax.broadcasted_iota(jnp.int32, (LANES, t), 0)
    qfeat = jnp.where(row == 0, c_hi, jnp.where(row == 1, ATT_POS * c_hi,
                      jnp.where(row == 2, c_lo, jnp.where(row == 3, ATT_POS * c_lo, 0.0))))
    qa = jnp.concatenate([jnp.where(row < QK_DIM, qt, 0.0), qfeat], axis=0)
    qb = jnp.concatenate([jnp.where(row >= QK_DIM, qt, 0.0), qfeat], axis=0)
    qts = jnp.concatenate([qa, qb], axis=1).astype(BF16)

    krow = lax.broadcasted_iota(jnp.int32, (t, LANES), 0)
    klane = lax.broadcasted_iota(jnp.int32, (t, LANES), 1)
    pbits = ATT_POS.bit_length() - 1
    kmod = (krow & (ATT_POS - 1)).astype(F32)
    kdiv = (krow >> pbits).astype(F32)
    kfeat = jnp.where((klane == 0) | (klane == 2), kmod,
                      jnp.where((klane == 1) | (klane == 3), kdiv, 0.0)).astype(BF16)

    m_scr[...] = jnp.full(m_scr.shape, NEG_BIG, F32)
    acc_scr[...] = jnp.zeros(acc_scr.shape, F32)

    def scores(j, s_ref):
        kj = k_ref[0, pl.ds(pl.multiple_of(j * t, t), t), :]
        s_ref[...] = jnp.dot(jnp.concatenate([kj, kfeat], axis=1), qts, preferred_element_type=F32)

    def colmax(s):
        while s.shape[0] > 8:
            half = s.shape[0] // 2
            s = jnp.maximum(s[:half], s[half:])
        return jnp.max(s, axis=0, keepdims=True)

    def softmax_pv(j, s_ref, corr):
        s = s_ref[...]
        if corr is not None:
            s = s + corr
        m_old = m_scr[...] - step_dec
        m_new = jnp.maximum(m_old, colmax(s))
        alpha = jnp.exp2(m_old - m_new)
        p = jnp.exp2(s - m_new).astype(BF16)
        pv = jnp.dot(vt_scr[j], p, preferred_element_type=F32)
        acc_scr[...] = alpha * acc_scr[...] + pv
        m_scr[...] = m_new

    def near(s_ref):
        corr = corr_scr[...]
        softmax_pv(qi, s_ref, jnp.concatenate([corr, corr], axis=1))

    scores(0, sa_scr)

    def pair(i, carry):
        scores(2 * i + 1, sb_scr)
        softmax_pv(2 * i, sa_scr, None)
        scores(2 * i + 2, sa_scr)
        softmax_pv(2 * i + 1, sb_scr, None)
        return carry

    def quad(i, carry):
        return pair(2 * i + 1, pair(2 * i, carry))

    def octo(i, carry):
        return quad(2 * i + 1, quad(2 * i, carry))

    lax.fori_loop(0, qi // 8, octo, 0)
    lax.fori_loop(2 * (qi // 8), qi // 4, quad, 0)
    lax.fori_loop(2 * (qi // 4), qi // 2, pair, 0)

    @pl.when(qi % 2 == 1)
    def _():
        scores(qi, sb_scr)
        softmax_pv(qi - 1, sa_scr, None)
        near(sb_scr)

    @pl.when(qi % 2 == 0)
    def _():
        near(sa_scr)

    lam = (jnp.exp(jnp.sum(lq1_ref[...] * lk1_ref[...], axis=-1, keepdims=True))
           - jnp.exp(jnp.sum(lq2_ref[...] * lk2_ref[...], axis=-1, keepdims=True)) + lam_init)
    o1 = acc_scr[0:V_DIM, 0:t] / acc_scr[V_DIM:V_DIM + 1, 0:t]
    o2 = acc_scr[0:V_DIM, t:2 * t] / acc_scr[V_DIM:V_DIM + 1, t:2 * t]
    ot = o1 - lam * o2
    ot = ot * lax.rsqrt(jnp.mean(ot * ot, axis=0, keepdims=True) + SUBLN_EPS)
    o_ref[...] = (ot.T * gsub_ref[...] * (1.0 - lam_init)).astype(BF16)


def _attn(slopes, lq1, lk1, lq2, lk2, proj, gsub, lam_init, t=512):
    s = proj.shape[1]
    smem = pl.BlockSpec(memory_space=pltpu.SMEM)
    vec = lambda n: pl.BlockSpec((1, n), lambda h, i: (0, 0))
    return pl.pallas_call(
        functools.partial(_attn_kernel, t=t, lam_init=lam_init),
        out_shape=jax.ShapeDtypeStruct((s, N_HEADS * V_DIM), BF16),
        grid=(N_HEADS, s // t),
        in_specs=[smem, vec(QK_DIM), vec(QK_DIM), vec(QK_DIM), vec(QK_DIM),
                  pl.BlockSpec((1, t, LANES), lambda h, i: (h, i, 0)),
                  pl.BlockSpec((1, s, LANES), lambda h, i: (N_HEADS + h, 0, 0)),
                  pl.BlockSpec((1, s, LANES), lambda h, i: (2 * N_HEADS + h, 0, 0)),
                  vec(V_DIM)],
        out_specs=pl.BlockSpec((t, V_DIM), lambda h, i: (i, h)),
        scratch_shapes=[pltpu.VMEM((s // t, ATT_VROWS, t), BF16),
                        pltpu.VMEM((t, 2 * t), F32), pltpu.VMEM((t, 2 * t), F32),
                        pltpu.VMEM((1, 2 * t), F32),
                        pltpu.VMEM((ATT_VROWS, 2 * t), F32),
                        pltpu.VMEM((t, t), F32)],
        compiler_params=pltpu.CompilerParams(dimension_semantics=("arbitrary", "arbitrary"),
                                             vmem_limit_bytes=VMEM_LIMIT),
        name="attn",
    )(slopes, lq1, lk1, lq2, lk2, proj, proj, proj, gsub)


def _route(logits):
    lane = lax.broadcasted_iota(jnp.int32, logits.shape, 1).astype(F32)
    far = float(4 * LANES)
    gl = jnp.where(lane < N_GROUPS, logits, NEG_BIG)
    gmax = jnp.max(gl, axis=-1, keepdims=True)
    g_w = 1.0 / jnp.sum(jnp.exp(gl - gmax), axis=-1, keepdims=True)
    g_idx = jnp.min(jnp.where(gl == gmax, lane, far), axis=-1, keepdims=True)
    lo = N_GROUPS + EXPERTS_PER_GROUP * g_idx
    el = jnp.where((lane >= lo) & (lane < lo + EXPERTS_PER_GROUP), logits, NEG_BIG)
    e1 = jnp.max(el, axis=-1, keepdims=True)
    i1 = jnp.min(jnp.where(el == e1, lane, far), axis=-1, keepdims=True)
    el2 = jnp.where(lane == i1, NEG_BIG, el)
    e2 = jnp.max(el2, axis=-1, keepdims=True)
    i2 = jnp.min(jnp.where(el2 == e2, lane, far), axis=-1, keepdims=True)
    r = jnp.exp(e2 - e1)
    w1 = 1.0 / (1.0 + r)
    w2 = r * w1
    out = jnp.where(lane == 0.0, i1 - N_GROUPS, 0.0)
    out = jnp.where(lane == 1.0, i2 - N_GROUPS, out)
    out = jnp.where(lane == 2.0, g_w * w1, out)
    out = jnp.where(lane == 3.0, g_w * w2, out)
    return out


def _outproj_kernel(ya_ref, b_ref, c_ref, u_ref, x_ref, wconv_ref, wout_ref, gpost_ref, gt_ref,
                    gpre_ref, sc_ref, sh_ref, wr_ref, br_ref,
                    x1_ref, h2_ref, rinfo_ref, zbuf, ybuf, *, tm, nc):
    i = pl.program_id(0)
    halo = 8

    @pl.when(i == 0)
    def _():
        zbuf[0:halo, :] = jnp.zeros((halo, zbuf.shape[1]), F32)

    @pl.when(i > 0)
    def _():
        zbuf[0:halo, :] = zbuf[tm:tm + halo, :]

    for c in range(nc):
        cols = slice(c * LANES, (c + 1) * LANES)
        zbuf[halo:halo + tm, cols] = c_ref[c].astype(F32) * u_ref[c].astype(F32)
    d_attn = ya_ref.shape[1]
    nh = 2
    hm = tm // nh
    for part in range(nh):
        r0 = part * hm
        rows = slice(r0, r0 + hm)
        ybuf[rows, 0:d_attn] = ya_ref[rows, :]
        for c in range(nc):
            cols = slice(c * LANES, (c + 1) * LANES)
            conv = (wconv_ref[0:1, cols] * zbuf[r0 + halo - 2:r0 + halo - 2 + hm, cols]
                    + wconv_ref[1:2, cols] * zbuf[r0 + halo - 1:r0 + halo - 1 + hm, cols]
                    + wconv_ref[2:3, cols] * zbuf[r0 + halo:r0 + halo + hm, cols])
            ybuf[rows, d_attn + c * LANES:d_attn + (c + 1) * LANES] = (
                b_ref[c, rows, :].astype(F32) * conv).astype(BF16)

        y = jnp.dot(ybuf[rows, :], wout_ref[...], preferred_element_type=F32)
        x1 = x_ref[rows, :] + gt_ref[...] * (_rms(y, NORM_EPS) * gpost_ref[...])
        x1_ref[rows, :] = x1
        h2 = (_rms(x1, NORM_EPS) * gpre_ref[...]) * (1.0 + sc_ref[...]) + sh_ref[...]
        h2_ref[rows, :] = h2
        logits = jnp.dot(h2.astype(BF16), wr_ref[...], preferred_element_type=F32) + br_ref[...]
        rinfo_ref[rows, :] = _route(logits)


def _outproj(y_attn, proj, x2, w_conv, wout_bf, gpost, gt, gpre, sc, sh, wr_bf, br, tm=512):
    s, d = x2.shape
    d_attn = y_attn.shape[1]
    nc = (d - d_attn) // LANES
    base = 3 * d_attn // LANES
    row = lambda n: pl.BlockSpec((1, n), lambda i: (0, 0))
    blk = lambda g: pl.BlockSpec((nc, tm, LANES), lambda i, g=g: (base // nc + g, i, 0))
    return pl.pallas_call(
        functools.partial(_outproj_kernel, tm=tm, nc=nc),
        out_shape=(jax.ShapeDtypeStruct((s, d), F32),
                   jax.ShapeDtypeStruct((s, d), F32),
                   jax.ShapeDtypeStruct((s, LANES), F32)),
        grid=(s // tm,),
        in_specs=[pl.BlockSpec((tm, d_attn), lambda i: (i, 0)),
                  blk(0), blk(1), blk(2),
                  pl.BlockSpec((tm, d), lambda i: (i, 0)),
                  pl.BlockSpec((CONV_WIDTH, d - d_attn), lambda i: (0, 0)),
                  pl.BlockSpec((d, d), lambda i: (0, 0)),
                  row(d), row(d), row(d), row(d), row(d),
                  pl.BlockSpec((d, LANES), lambda i: (0, 0)),
                  row(LANES)],
        out_specs=(pl.BlockSpec((tm, d), lambda i: (i, 0)),
                   pl.BlockSpec((tm, d), lambda i: (i, 0)),
                   pl.BlockSpec((tm, LANES), lambda i: (i, 0))),
        scratch_shapes=[pltpu.VMEM((tm + 8, d - d_attn), F32), pltpu.VMEM((tm, d), BF16)],
        compiler_params=pltpu.CompilerParams(dimension_semantics=("arbitrary",),
                                             vmem_limit_bytes=VMEM_LIMIT),
        name="outproj",
    )(y_attn, proj, proj, proj, x2, w_conv, wout_bf, gpost, gt, gpre, sc, sh, wr_bf, br)


MOE_TILE = 256


def _moe_num_tiles(s):
    return (2 * s) // MOE_TILE + N_EXPERTS


def _plan_kernel(rinfo_ref, pos_ref, tinfo_ref, cnt_scr, base_scr, *, tm, ntm):
    ph = pl.program_id(0)
    i = pl.program_id(1)
    r = rinfo_ref[...]
    lane = lax.broadcasted_iota(jnp.int32, r.shape, 1).astype(F32)
    oh0 = jnp.where(lane == r[:, 0:1], 1.0, 0.0)
    oh1 = jnp.where(lane == r[:, 1:2], 1.0, 0.0)
    both = oh0 + oh1
    tile_cnt = jnp.sum(both, axis=0, keepdims=True)

    @pl.when((ph == 0) & (i == 0))
    def _():
        cnt_scr[...] = jnp.zeros_like(cnt_scr)

    @pl.when(ph == 0)
    def _():
        cnt_scr[...] += tile_cnt

    @pl.when((ph == 1) & (i == 0))
    def _():
        ktiles = jnp.floor((cnt_scr[...] + (MOE_TILE - 1)) * (1.0 / MOE_TILE))
        rr = lax.broadcasted_iota(jnp.int32, (LANES, LANES), 0)
        cc = lax.broadcasted_iota(jnp.int32, (LANES, LANES), 1)
        upper = jnp.where(rr < cc, 1.0, 0.0).astype(BF16)
        first = jnp.dot(jnp.broadcast_to(ktiles, (8, LANES)).astype(BF16), upper,
                        preferred_element_type=F32)[0:1, :]
        base_scr[...] = first * MOE_TILE
        ti = lax.broadcasted_iota(jnp.int32, (ntm, LANES), 0).astype(F32)
        el = lax.broadcasted_iota(jnp.int32, (ntm, LANES), 1).astype(F32)
        owned = jnp.where(ti >= first, jnp.where(ti < first + ktiles, 1.0, 0.0), 0.0)
        texp = jnp.sum(owned * el, axis=1, keepdims=True)
        nact = jnp.sum(ktiles, axis=1, keepdims=True)
        tinfo_ref[...] = jnp.where(el == 0.0, texp, jnp.where(el == 1.0, nact, 0.0))

    @pl.when(ph == 1)
    def _():
        rr = lax.broadcasted_iota(jnp.int32, (tm, tm), 0)
        cc = lax.broadcasted_iota(jnp.int32, (tm, tm), 1)
        lower = jnp.where(cc < rr, 1.0, 0.0).astype(BF16)
        before = jnp.dot(lower, both.astype(BF16), preferred_element_type=F32)
        val = before + base_scr[...]
        p0 = jnp.sum(oh0 * val, axis=1, keepdims=True)
        p1 = jnp.sum(oh1 * val, axis=1, keepdims=True)
        pos_ref[...] = jnp.where(lane == 0.0, p0, jnp.where(lane == 1.0, p1, 0.0))
        base_scr[...] += tile_cnt


def _plan(rinfo, tm=512):
    s = rinfo.shape[0]
    ntm = _moe_num_tiles(s)
    return pl.pallas_call(
        functools.partial(_plan_kernel, tm=tm, ntm=ntm),
        out_shape=(jax.ShapeDtypeStruct((s, LANES), F32), jax.ShapeDtypeStruct((ntm, LANES), F32)),
        grid=(2, s // tm),
        in_specs=[pl.BlockSpec((tm, LANES), lambda p, i: (i, 0))],
        out_specs=(pl.BlockSpec((tm, LANES), lambda p, i: (i * p, 0)),
                   pl.BlockSpec((ntm, LANES), lambda p, i: (0, 0))),
        scratch_shapes=[pltpu.VMEM((1, LANES), F32), pltpu.VMEM((1, LANES), F32)],
        compiler_params=pltpu.CompilerParams(dimension_semantics=("arbitrary", "arbitrary"),
                                             vmem_limit_bytes=VMEM_LIMIT),
        name="plan",
    )(rinfo)


def _dispatch_kernel(pos_ref, texp_ref, nact_ref, h2_ref, xs_ref, zero_scr, sem, zsem, *, tm):
    i = pl.program_id(0)

    @pl.when(i == 0)
    def _():
        zero_scr[...] = jnp.zeros_like(zero_scr)
        nact = nact_ref[0]
        ntm = xs_ref.shape[0] // MOE_TILE

        def is_last(t):
            nxt = texp_ref[jnp.minimum(t + 1, nact - 1)]
            return (t >= nact - 1) | (nxt != texp_ref[jnp.minimum(t, nact - 1)])

        def zstart(t, carry):
            @pl.when(is_last(t))
            def _():
                pltpu.make_async_copy(zero_scr, xs_ref.at[pl.ds(pl.multiple_of(t * MOE_TILE, MOE_TILE), MOE_TILE), :],
                                      zsem).start()
            return carry

        def zwait(t, carry):
            @pl.when(is_last(t))
            def _():
                pltpu.make_async_copy(zero_scr, xs_ref.at[pl.ds(0, MOE_TILE), :], zsem).wait()
            return carry

        lax.fori_loop(0, ntm, zstart, 0)
        lax.fori_loop(0, ntm, zwait, 0)

    def start(r, carry):
        tok = i * tm + r
        for slot in range(2):
            p = pos_ref[2 * tok + slot]
            pltpu.make_async_copy(h2_ref.at[pl.ds(r, 1), :], xs_ref.at[pl.ds(p, 1), :], sem).start()
        return carry

    lax.fori_loop(0, tm, start, 0, unroll=8)
    for slot in range(2):
        pltpu.make_async_copy(h2_ref, xs_ref.at[pl.ds(0, tm), :], sem).wait()


def _dispatch(pos_flat, texp, nact, h2, tm=512):
    s, d = h2.shape
    ntm = _moe_num_tiles(s)
    return pl.pallas_call(
        functools.partial(_dispatch_kernel, tm=tm),
        out_shape=jax.ShapeDtypeStruct((ntm * MOE_TILE, d), F32),
        grid_spec=pltpu.PrefetchScalarGridSpec(
            num_scalar_prefetch=3,
            grid=(s // tm,),
            in_specs=[pl.BlockSpec((tm, d), lambda i, *_: (i, 0))],
            out_specs=pl.BlockSpec(memory_space=pl.ANY),
            scratch_shapes=[pltpu.VMEM((MOE_TILE, d), F32), pltpu.SemaphoreType.DMA, pltpu.SemaphoreType.DMA]),
        compiler_params=pltpu.CompilerParams(dimension_semantics=("arbitrary",),
                                             vmem_limit_bytes=VMEM_LIMIT),
        name="dispatch",
    )(pos_flat, texp, nact, h2)


def _experts_kernel(texp_ref, nact_ref, xs_ref, wg_hbm, wu_hbm, wd_hbm, ys_ref,
                    wg_f32, wu_f32, wd_f32, wg_bf, wu_bf, wd_bf, ord_scr, sem):
    i = pl.program_id(0)
    nact = nact_ref[0]
    j = jnp.minimum(i, nact - 1)
    e = texp_ref[j]
    fresh = (i == 0) | (e != texp_ref[jnp.maximum(j - 1, 0)])

    def weight_copies(expert, slot):
        return (pltpu.make_async_copy(wg_hbm.at[expert], wg_f32.at[slot], sem.at[slot]),
                pltpu.make_async_copy(wu_hbm.at[expert], wu_f32.at[slot], sem.at[slot]),
                pltpu.make_async_copy(wd_hbm.at[expert], wd_f32.at[slot], sem.at[slot]))

    @pl.when(i == 0)
    def _():
        ord_scr[0] = 0
        for c in weight_copies(e, 0):
            c.start()

    @pl.when((i < nact) & fresh)
    def _():
        slot = ord_scr[0] % 2
        for c in weight_copies(e, slot):
            c.wait()
        wg_bf[...] = wg_f32[slot].astype(BF16)
        wu_bf[...] = wu_f32[slot].astype(BF16)
        wd_bf[...] = wd_f32[slot].astype(BF16)
        nxt = lax.while_loop(lambda t: (t < nact) & (texp_ref[jnp.minimum(t, nact - 1)] == e),
                             lambda t: t + 1, i + 1)

        @pl.when(nxt < nact)
        def _():
            for c in weight_copies(texp_ref[nxt], 1 - slot):
                c.start()
        ord_scr[0] = ord_scr[0] + 1

    @pl.when(i < nact)
    def _():
        x = xs_ref[...].astype(BF16)
        hg = jnp.dot(x, wg_bf[...], preferred_element_type=F32)
        hu = jnp.dot(x, wu_bf[...], preferred_element_type=F32)
        a = (hg * _sigmoid(hg)) * hu
        ys_ref[...] = jnp.dot(a.astype(BF16), wd_bf[...], preferred_element_type=F32)

    @pl.when(i >= nact)
    def _():
        ys_ref[...] = jnp.zeros_like(ys_ref)


def _experts(texp, nact, xs, w_gate, w_up, w_down):
    p, d = xs.shape
    ntm = p // MOE_TILE
    _, _, f = w_gate.shape

    def tile(i, texp_ref, nact_ref):
        return (jnp.minimum(i, nact_ref[0] - 1), 0)

    hbm = pl.BlockSpec(memory_space=pl.ANY)
    return pl.pallas_call(
        _experts_kernel,
        out_shape=jax.ShapeDtypeStruct((p, d), F32),
        grid_spec=pltpu.PrefetchScalarGridSpec(
            num_scalar_prefetch=2,
            grid=(ntm,),
            in_specs=[pl.BlockSpec((MOE_TILE, d), tile), hbm, hbm, hbm],
            out_specs=pl.BlockSpec((MOE_TILE, d), lambda i, *_: (i, 0)),
            scratch_shapes=[pltpu.VMEM((2, d, f), F32), pltpu.VMEM((2, d, f), F32), pltpu.VMEM((2, f, d), F32),
                            pltpu.VMEM((d, f), BF16), pltpu.VMEM((d, f), BF16), pltpu.VMEM((f, d), BF16),
                            pltpu.SMEM((1,), jnp.int32), pltpu.SemaphoreType.DMA((2,))]),
        compiler_params=pltpu.CompilerParams(dimension_semantics=("arbitrary",),
                                             vmem_limit_bytes=VMEM_LIMIT),
        name="experts",
    )(texp, nact, xs, w_gate, w_up, w_down)


def _combine_kernel(pos_ref, ys_ref, rinfo_ref, x1_ref, gpost_ref, gt_ref, o_ref, ybuf, sem, *, tm):
    i = pl.program_id(0)

    def gather(tile, b):
        def start(r, carry):
            tok = tile * tm + r
            for slot in range(2):
                p = pos_ref[2 * tok + slot]
                pltpu.make_async_copy(ys_ref.at[pl.ds(p, 1), :], ybuf.at[b, slot, pl.ds(r, 1), :],
                                      sem.at[b]).start()
            return carry
        lax.fori_loop(0, tm, start, 0, unroll=8)

    @pl.when(i == 0)
    def _():
        gather(0, 0)

    @pl.when(i + 1 < pl.num_programs(0))
    def _():
        gather(i + 1, (i + 1) % 2)

    b = i % 2
    for slot in range(2):
        pltpu.make_async_copy(ys_ref.at[pl.ds(0, tm), :], ybuf.at[b, slot], sem.at[b]).wait()
    r = rinfo_ref[...]
    y = r[:, 2:3] * ybuf[b, 0] + r[:, 3:4] * ybuf[b, 1]
    o_ref[...] = x1_ref[...] + gt_ref[...] * (_rms(y, NORM_EPS) * gpost_ref[...])


def _combine(pos_flat, ys, rinfo, x1, gpost, gt, tm=512):
    s, d = x1.shape
    row = pl.BlockSpec((1, d), lambda i, *_: (0, 0))
    return pl.pallas_call(
        functools.partial(_combine_kernel, tm=tm),
        out_shape=jax.ShapeDtypeStruct((s, d), F32),
        grid_spec=pltpu.PrefetchScalarGridSpec(
            num_scalar_prefetch=1,
            grid=(s // tm,),
            in_specs=[pl.BlockSpec(memory_space=pl.ANY),
                      pl.BlockSpec((tm, LANES), lambda i, *_: (i, 0)),
                      pl.BlockSpec((tm, d), lambda i, *_: (i, 0)),
                      row, row],
            out_specs=pl.BlockSpec((tm, d), lambda i, *_: (i, 0)),
            scratch_shapes=[pltpu.VMEM((2, 2, tm, d), F32), pltpu.SemaphoreType.DMA((2,))]),
        compiler_params=pltpu.CompilerParams(dimension_semantics=("arbitrary",),
                                             vmem_limit_bytes=VMEM_LIMIT),
        name="combine",
    )(pos_flat, ys, rinfo, x1, gpost, gt)


def _moe(h2, rinfo, w_gate, w_up, w_down, x1, gpost, gt):
    pos, tinfo = _plan(rinfo)
    pos_flat = pos[:, 0:2].astype(jnp.int32).reshape(-1)
    texp = tinfo[:, 0].astype(jnp.int32)
    nact = tinfo[0:1, 1].astype(jnp.int32)
    xs = _dispatch(pos_flat, texp, nact, h2)
    ys = _experts(texp, nact, xs, w_gate, w_up, w_down)
    return _combine(pos_flat, ys, rinfo, x1, gpost, gt)


def kernel(x, c, w_ada, b_ada, g_pre1, w_in, lam_q1, lam_k1, lam_q2, lam_k2, g_subln, w_conv, w_out,
           g_post1, g_pre2, w_router_g, b_router_g, w_router_e, b_router_e, w_gate, w_up, w_down, g_post2):
    b, s, d = x.shape
    assert b == 1
    depth = w_ada.shape[0]
    slopes = jnp.asarray([2.0 ** (-8.0 * (i + 1) / N_HEADS) for i in range(N_HEADS)], F32)
    x2 = x.reshape(s, d)
    for l in range(depth):
        lam_init = 0.8 - 0.6 * math.exp(-0.3 * l)
        mod = _ada(c.reshape(d, 1), w_ada[l], b_ada[l].reshape(1, -1))
        sh1, sc1, gt1, sh2, sc2, gt2 = [mod[:, k * d:(k + 1) * d] for k in range(6)]
        proj = _inproj(x2, g_pre1[l].reshape(1, d), sc1, sh1, w_in[l])
        y_attn = _attn(slopes, lam_q1[l].reshape(1, -1), lam_k1[l].reshape(1, -1),
                       lam_q2[l].reshape(1, -1), lam_k2[l].reshape(1, -1), proj,
                       g_subln[l].reshape(1, -1), lam_init)
        wr = jnp.concatenate([w_router_g[l], w_router_e[l].reshape(d, N_EXPERTS),
                              jnp.zeros((d, LANES - N_GROUPS - N_EXPERTS), F32)], axis=1).astype(BF16)
        br = jnp.concatenate([b_router_g[l], b_router_e[l].reshape(N_EXPERTS),
                              jnp.zeros((LANES - N_GROUPS - N_EXPERTS,), F32)]).reshape(1, LANES)
        x1, h2, rinfo = _outproj(y_attn, proj, x2, w_conv[l], w_out[l].astype(BF16),
                                 g_post1[l].reshape(1, d), gt1, g_pre2[l].reshape(1, d), sc2, sh2, wr, br)
        x2 = _moe(h2, rinfo, w_gate[l], w_up[l], w_down[l], x1, g_post2[l].reshape(1, d), gt2)
    return x2.reshape(b, s, d)
```

```python
import functools
import math

import jax
import jax.numpy as jnp
from jax import lax
from jax.experimental import pallas as pl
from jax.experimental.pallas import tpu as pltpu

F32 = jnp.float32
BF16 = jnp.bfloat16

LANES = 128
CHUNK = 64
N_HEADS = 8
QK_DIM = 64
V_DIM = 128
CONV_WIDTH = 3
N_GROUPS = 4
EXPERTS_PER_GROUP = 8
N_EXPERTS = N_GROUPS * EXPERTS_PER_GROUP
NORM_EPS = 1e-6
SUBLN_EPS = 1e-5
NEG_BIG = -1e30
LOG2E = 1.4426950408889634
VMEM_LIMIT = 56 * 1024 * 1024


def _rms(x, eps):
    return x * lax.rsqrt(jnp.mean(x * x, axis=-1, keepdims=True) + eps)


def _sigmoid(x):
    return 1.0 / (1.0 + jnp.exp(-x))


def _ada_kernel(c_ref, w_ref, b_ref, o_ref):
    c = c_ref[...]
    s = c * _sigmoid(c)
    o_ref[...] = jnp.sum(w_ref[...] * s, axis=0, keepdims=True) + b_ref[...]


def _ada(c_col, w_ada, b_ada, tn=1024):
    d, n = w_ada.shape
    return pl.pallas_call(
        _ada_kernel,
        out_shape=jax.ShapeDtypeStruct((1, n), F32),
        grid=(n // tn,),
        in_specs=[pl.BlockSpec((d, 1), lambda j: (0, 0)),
                  pl.BlockSpec((d, tn), lambda j: (0, j)),
                  pl.BlockSpec((1, tn), lambda j: (0, j))],
        out_specs=pl.BlockSpec((1, tn), lambda j: (0, j)),
        compiler_params=pltpu.CompilerParams(dimension_semantics=("arbitrary",),
                                             vmem_limit_bytes=VMEM_LIMIT),
        name="ada",
    )(c_col, w_ada, b_ada)


def _inproj_kernel(x_ref, g_ref, sc_ref, sh_ref, w_ref, o_ref, h_scr, *, cn):
    @pl.when(pl.program_id(1) == 0)
    def _():
        rows = 16

        scale = g_ref[...] * (1.0 + sc_ref[...])

        def norm_rows(r, carry):
            sl = pl.ds(pl.multiple_of(r * rows, rows), rows)
            h_scr[sl, :] = (_rms(x_ref[sl, :], NORM_EPS) * scale + sh_ref[...]).astype(BF16)
            return carry
        lax.fori_loop(0, x_ref.shape[0] // rows, norm_rows, 0, unroll=8)

    acc = jnp.dot(h_scr[...], w_ref[...].astype(BF16), preferred_element_type=F32)
    for c in range(cn):
        o_ref[c] = acc[:, c * LANES:(c + 1) * LANES].astype(BF16)


def _inproj(x2, g, sc, sh, w, tm=1024, cn=4):
    s, d = x2.shape
    n = w.shape[1]
    nchunks = n // LANES
    row = lambda i, j: (0, 0)
    return pl.pallas_call(
        functools.partial(_inproj_kernel, cn=cn),
        out_shape=jax.ShapeDtypeStruct((nchunks, s, LANES), BF16),
        grid=(s // tm, nchunks // cn),
        in_specs=[pl.BlockSpec((tm, d), lambda i, j: (i, 0)),
                  pl.BlockSpec((1, d), row),
                  pl.BlockSpec((1, d), row),
                  pl.BlockSpec((1, d), row),
                  pl.BlockSpec((d, cn * LANES), lambda i, j: (0, j))],
        out_specs=pl.BlockSpec((cn, tm, LANES), lambda i, j: (j, i, 0)),
        scratch_shapes=[pltpu.VMEM((tm, d), BF16)],
        compiler_params=pltpu.CompilerParams(dimension_semantics=("arbitrary", "arbitrary"),
                                             vmem_limit_bytes=VMEM_LIMIT),
        name="inproj",
    )(x2, g, sc, sh, w)


ATT_POS = 256
ATT_VROWS = 144


def _attn_kernel(slope_ref, lq1_ref, lk1_ref, lq2_ref, lk2_ref, q_ref, k_ref, v_ref, gsub_ref,
                 o_ref, vt_scr, sa_scr, sb_scr, m_scr, acc_scr, corr_scr, *, t, lam_init):
    h = pl.program_id(0)
    nkt = vt_scr.shape[0]
    slope2 = slope_ref[h] * LOG2E
    c_hi = jnp.full((LANES, t), slope2, F32).astype(BF16).astype(F32)
    c_lo = (jnp.full((LANES, t), slope2, F32) - c_hi).astype(BF16).astype(F32)

    ones_row = jnp.where(lax.broadcasted_iota(jnp.int32, (ATT_VROWS - V_DIM, t), 0) == 0, 1.0, 0.0)

    def tr(c, carry):
        vc = v_ref[0, pl.ds(pl.multiple_of(c * t, t), t), :]
        vt_scr[c] = jnp.concatenate([vc.astype(F32).T, ones_row], axis=0).astype(BF16)
        return carry
    lax.fori_loop(0, nkt, tr, 0)

    krel = lax.broadcasted_iota(jnp.int32, (t, t), 0)
    qrel = lax.broadcasted_iota(jnp.int32, (t, t), 1)
    shift = CHUNK.bit_length() - 1
    allowed = (krel >> shift) <= (qrel >> shift)
    c_mxu = (c_hi + c_lo)[0:1, 0:1]
    corr_scr[...] = jnp.where(allowed, slope2 * (qrel - jnp.abs(qrel - krel)).astype(F32)
                              - c_mxu * krel.astype(F32), NEG_BIG)

    step_dec = slope2 * t
    row = lax.broadcasted_iota(jnp.int32, (LANES, t), 0)
    qfeat = jnp.where(row == 0, c_hi, jnp.where(row == 1, ATT_POS * c_hi,
                      jnp.where(row == 2, c_lo, jnp.where(row == 3, ATT_POS * c_lo, 0.0))))
    lam = (jnp.exp(jnp.sum(lq1_ref[...] * lk1_ref[...], axis=-1, keepdims=True))
           - jnp.exp(jnp.sum(lq2_ref[...] * lk2_ref[...], axis=-1, keepdims=True)) + lam_init)

    def colmax(s):
        while s.shape[0] > 8:
            half = s.shape[0] // 2
            s = jnp.maximum(s[:half], s[half:])
        return jnp.max(s, axis=0, keepdims=True)

    def query_tile(qi, carry):
        rows = pl.ds(pl.multiple_of(qi * t, t), t)
        qt = (q_ref[0, rows, :].astype(F32) * (LOG2E * QK_DIM ** -0.5)).T
        qa = jnp.concatenate([jnp.where(row < QK_DIM, qt, 0.0), qfeat], axis=0)
        qb = jnp.concatenate([jnp.where(row >= QK_DIM, qt, 0.0), qfeat], axis=0)
        qts = jnp.concatenate([qa, qb], axis=1).astype(BF16)

        krow = lax.broadcasted_iota(jnp.int32, (t, LANES), 0)
        klane = lax.broadcasted_iota(jnp.int32, (t, LANES), 1)
        pbits = ATT_POS.bit_length() - 1
        kmod = (krow & (ATT_POS - 1)).astype(F32)
        kdiv = (krow >> pbits).astype(F32)
        kfeat = jnp.where((klane == 0) | (klane == 2), kmod,
                          jnp.where((klane == 1) | (klane == 3), kdiv, 0.0)).astype(BF16)

        m_scr[...] = jnp.full(m_scr.shape, NEG_BIG, F32)
        acc_scr[...] = jnp.zeros(acc_scr.shape, F32)

        def scores(j, s_ref):
            kj = k_ref[0, pl.ds(pl.multiple_of(j * t, t), t), :]
            s_ref[...] = jnp.dot(jnp.concatenate([kj, kfeat], axis=1), qts, preferred_element_type=F32)

        def softmax_pv(j, s_ref, corr):
            s = s_ref[...]
            if corr is not None:
                s = s + corr
            m_old = m_scr[...] - step_dec
            m_new = jnp.maximum(m_old, colmax(s))
            alpha = jnp.exp2(m_old - m_new)
            p = jnp.exp2(s - m_new).astype(BF16)
            pv = jnp.dot(vt_scr[j], p, preferred_element_type=F32)
            acc_scr[...] = alpha * acc_scr[...] + pv
            m_scr[...] = m_new

        def near(s_ref):
            corr = corr_scr[...]
            softmax_pv(qi, s_ref, jnp.concatenate([corr, corr], axis=1))

        scores(0, sa_scr)

        def pair(i, c):
            scores(2 * i + 1, sb_scr)
            softmax_pv(2 * i, sa_scr, None)
            scores(2 * i + 2, sa_scr)
            softmax_pv(2 * i + 1, sb_scr, None)
            return c

        def quad(i, c):
            return pair(2 * i + 1, pair(2 * i, c))

        def octo(i, c):
            return quad(2 * i + 1, quad(2 * i, c))

        lax.fori_loop(0, qi // 8, octo, 0)
        lax.fori_loop(2 * (qi // 8), qi // 4, quad, 0)
        lax.fori_loop(2 * (qi // 4), qi // 2, pair, 0)

        @pl.when(qi % 2 == 1)
        def _():
            scores(qi, sb_scr)
            softmax_pv(qi - 1, sa_scr, None)
            near(sb_scr)

        @pl.when(qi % 2 == 0)
        def _():
            near(sa_scr)

        o1 = acc_scr[0:V_DIM, 0:t] / acc_scr[V_DIM:V_DIM + 1, 0:t]
        o2 = acc_scr[0:V_DIM, t:2 * t] / acc_scr[V_DIM:V_DIM + 1, t:2 * t]
        ot = o1 - lam * o2
        ot = ot * lax.rsqrt(jnp.mean(ot * ot, axis=0, keepdims=True) + SUBLN_EPS)
        o_ref[0, rows, :] = (ot.T * gsub_ref[...] * (1.0 - lam_init)).astype(BF16)
        return carry

    lax.fori_loop(0, nkt, query_tile, 0)


def _attn(slopes, lq1, lk1, lq2, lk2, proj, gsub, lam_init, t=512):
    s = proj.shape[1]
    smem = pl.BlockSpec(memory_space=pltpu.SMEM)
    vec = lambda n: pl.BlockSpec((1, n), lambda h: (0, 0))
    head = lambda base: pl.BlockSpec((1, s, LANES), lambda h, base=base: (base + h, 0, 0))
    return pl.pallas_call(
        functools.partial(_attn_kernel, t=t, lam_init=lam_init),
        out_shape=jax.ShapeDtypeStruct((N_HEADS, s, V_DIM), BF16),
        grid=(N_HEADS,),
        in_specs=[smem, vec(QK_DIM), vec(QK_DIM), vec(QK_DIM), vec(QK_DIM),
                  head(0), head(N_HEADS), head(2 * N_HEADS), vec(V_DIM)],
        out_specs=pl.BlockSpec((1, s, V_DIM), lambda h: (h, 0, 0)),
        scratch_shapes=[pltpu.VMEM((s // t, ATT_VROWS, t), BF16),
                        pltpu.VMEM((t, 2 * t), F32), pltpu.VMEM((t, 2 * t), F32),
                        pltpu.VMEM((1, 2 * t), F32),
                        pltpu.VMEM((ATT_VROWS, 2 * t), F32),
                        pltpu.VMEM((t, t), F32)],
        compiler_params=pltpu.CompilerParams(dimension_semantics=("arbitrary",),
                                             vmem_limit_bytes=VMEM_LIMIT),
        name="attn",
    )(slopes, lq1, lk1, lq2, lk2, proj, proj, proj, gsub)


def _route(logits):
    lane = lax.broadcasted_iota(jnp.int32, logits.shape, 1).astype(F32)
    far = float(4 * LANES)
    gl = jnp.where(lane < N_GROUPS, logits, NEG_BIG)
    gmax = jnp.max(gl, axis=-1, keepdims=True)
    g_w = 1.0 / jnp.sum(jnp.exp(gl - gmax), axis=-1, keepdims=True)
    g_idx = jnp.min(jnp.where(gl == gmax, lane, far), axis=-1, keepdims=True)
    lo = N_GROUPS + EXPERTS_PER_GROUP * g_idx
    el = jnp.where((lane >= lo) & (lane < lo + EXPERTS_PER_GROUP), logits, NEG_BIG)
    e1 = jnp.max(el, axis=-1, keepdims=True)
    i1 = jnp.min(jnp.where(el == e1, lane, far), axis=-1, keepdims=True)
    el2 = jnp.where(lane == i1, NEG_BIG, el)
    e2 = jnp.max(el2, axis=-1, keepdims=True)
    i2 = jnp.min(jnp.where(el2 == e2, lane, far), axis=-1, keepdims=True)
    r = jnp.exp(e2 - e1)
    w1 = 1.0 / (1.0 + r)
    w2 = r * w1
    out = jnp.where(lane == 0.0, i1 - N_GROUPS, 0.0)
    out = jnp.where(lane == 1.0, i2 - N_GROUPS, out)
    out = jnp.where(lane == 2.0, g_w * w1, out)
    out = jnp.where(lane == 3.0, g_w * w2, out)
    return out


def _outproj_kernel(ya_ref, b_ref, c_ref, u_ref, x_ref, wconv_ref, wout_ref, gpost_ref, gt_ref,
                    gpre_ref, sc_ref, sh_ref, wr_ref, br_ref,
                    x1_ref, h2_ref, rinfo_ref, zbuf, ybuf, *, tm, nc):
    i = pl.program_id(0)
    halo = 8

    @pl.when(i == 0)
    def _():
        zbuf[0:halo, :] = jnp.zeros((halo, zbuf.shape[1]), F32)

    @pl.when(i > 0)
    def _():
        zbuf[0:halo, :] = zbuf[tm:tm + halo, :]

    for c in range(nc):
        cols = slice(c * LANES, (c + 1) * LANES)
        zbuf[halo:halo + tm, cols] = c_ref[c].astype(F32) * u_ref[c].astype(F32)
    d_attn = ya_ref.shape[0] * LANES
    nh = 2
    hm = tm // nh
    for part in range(nh):
        r0 = part * hm
        rows = slice(r0, r0 + hm)
        for c in range(ya_ref.shape[0]):
            ybuf[rows, c * LANES:(c + 1) * LANES] = ya_ref[c, rows, :]
        for c in range(nc):
            cols = slice(c * LANES, (c + 1) * LANES)
            conv = (wconv_ref[0:1, cols] * zbuf[r0 + halo - 2:r0 + halo - 2 + hm, cols]
                    + wconv_ref[1:2, cols] * zbuf[r0 + halo - 1:r0 + halo - 1 + hm, cols]
                    + wconv_ref[2:3, cols] * zbuf[r0 + halo:r0 + halo + hm, cols])
            ybuf[rows, d_attn + c * LANES:d_attn + (c + 1) * LANES] = (
                b_ref[c, rows, :].astype(F32) * conv).astype(BF16)

        y = jnp.dot(ybuf[rows, :], wout_ref[...], preferred_element_type=F32)
        x1 = x_ref[rows, :] + gt_ref[...] * (_rms(y, NORM_EPS) * gpost_ref[...])
        x1_ref[rows, :] = x1
        h2 = (_rms(x1, NORM_EPS) * gpre_ref[...]) * (1.0 + sc_ref[...]) + sh_ref[...]
        h2_ref[rows, :] = h2
        logits = jnp.dot(h2.astype(BF16), wr_ref[...], preferred_element_type=F32) + br_ref[...]
        rinfo_ref[rows, :] = _route(logits)


def _outproj(y_attn, proj, x2, w_conv, wout_bf, gpost, gt, gpre, sc, sh, wr_bf, br, tm=512):
    s, d = x2.shape
    nh = y_attn.shape[0]
    d_attn = nh * LANES
    nc = (d - d_attn) // LANES
    base = 3 * d_attn // LANES
    row = lambda n: pl.BlockSpec((1, n), lambda i: (0, 0))
    blk = lambda g: pl.BlockSpec((nc, tm, LANES), lambda i, g=g: (base // nc + g, i, 0))
    return pl.pallas_call(
        functools.partial(_outproj_kernel, tm=tm, nc=nc),
        out_shape=(jax.ShapeDtypeStruct((s, d), F32),
                   jax.ShapeDtypeStruct((s, d), F32),
                   jax.ShapeDtypeStruct((s, LANES), F32)),
        grid=(s // tm,),
        in_specs=[pl.BlockSpec((nh, tm, LANES), lambda i: (0, i, 0)),
                  blk(0), blk(1), blk(2),
                  pl.BlockSpec((tm, d), lambda i: (i, 0)),
                  pl.BlockSpec((CONV_WIDTH, d - d_attn), lambda i: (0, 0)),
                  pl.BlockSpec((d, d), lambda i: (0, 0)),
                  row(d), row(d), row(d), row(d), row(d),
                  pl.BlockSpec((d, LANES), lambda i: (0, 0)),
                  row(LANES)],
        out_specs=(pl.BlockSpec((tm, d), lambda i: (i, 0)),
                   pl.BlockSpec((tm, d), lambda i: (i, 0)),
                   pl.BlockSpec((tm, LANES), lambda i: (i, 0))),
        scratch_shapes=[pltpu.VMEM((tm + 8, d - d_attn), F32), pltpu.VMEM((tm, d), BF16)],
        compiler_params=pltpu.CompilerParams(dimension_semantics=("arbitrary",),
                                             vmem_limit_bytes=VMEM_LIMIT),
        name="outproj",
    )(y_attn, proj, proj, proj, x2, w_conv, wout_bf, gpost, gt, gpre, sc, sh, wr_bf, br)


MOE_TILE = 256


def _moe_num_tiles(s):
    return (2 * s) // MOE_TILE + N_EXPERTS


def _plan_kernel(rinfo_ref, pos_ref, tinfo_ref, cnt_scr, base_scr, *, tm, ntm):
    ph = pl.program_id(0)
    i = pl.program_id(1)
    r = rinfo_ref[...]
    lane = lax.broadcasted_iota(jnp.int32, r.shape, 1).astype(F32)
    oh0 = jnp.where(lane == r[:, 0:1], 1.0, 0.0)
    oh1 = jnp.where(lane == r[:, 1:2], 1.0, 0.0)
    both = oh0 + oh1
    tile_cnt = jnp.sum(both, axis=0, keepdims=True)

    @pl.when((ph == 0) & (i == 0))
    def _():
        cnt_scr[...] = jnp.zeros_like(cnt_scr)

    @pl.when(ph == 0)
    def _():
        cnt_scr[...] += tile_cnt

    @pl.when((ph == 1) & (i == 0))
    def _():
        ktiles = jnp.floor((cnt_scr[...] + (MOE_TILE - 1)) * (1.0 / MOE_TILE))
        rr = lax.broadcasted_iota(jnp.int32, (LANES, LANES), 0)
        cc = lax.broadcasted_iota(jnp.int32, (LANES, LANES), 1)
        upper = jnp.where(rr < cc, 1.0, 0.0).astype(BF16)
        first = jnp.dot(jnp.broadcast_to(ktiles, (8, LANES)).astype(BF16), upper,
                        preferred_element_type=F32)[0:1, :]
        base_scr[...] = first * MOE_TILE
        ti = lax.broadcasted_iota(jnp.int32, (ntm, LANES), 0).astype(F32)
        el = lax.broadcasted_iota(jnp.int32, (ntm, LANES), 1).astype(F32)
        owned = jnp.where(ti >= first, jnp.where(ti < first + ktiles, 1.0, 0.0), 0.0)
        texp = jnp.sum(owned * el, axis=1, keepdims=True)
        nact = jnp.sum(ktiles, axis=1, keepdims=True)
        tinfo_ref[...] = jnp.where(el == 0.0, texp, jnp.where(el == 1.0, nact, 0.0))

    @pl.when(ph == 1)
    def _():
        rr = lax.broadcasted_iota(jnp.int32, (tm, tm), 0)
        cc = lax.broadcasted_iota(jnp.int32, (tm, tm), 1)
        lower = jnp.where(cc < rr, 1.0, 0.0).astype(BF16)
        before = jnp.dot(lower, both.astype(BF16), preferred_element_type=F32)
        val = before + base_scr[...]
        p0 = jnp.sum(oh0 * val, axis=1, keepdims=True)
        p1 = jnp.sum(oh1 * val, axis=1, keepdims=True)
        pos_ref[...] = jnp.where(lane == 0.0, p0, jnp.where(lane == 1.0, p1, 0.0))
        base_scr[...] += tile_cnt


def _plan(rinfo, tm=512):
    s = rinfo.shape[0]
    ntm = _moe_num_tiles(s)
    return pl.pallas_call(
        functools.partial(_plan_kernel, tm=tm, ntm=ntm),
        out_shape=(jax.ShapeDtypeStruct((s, LANES), F32), jax.ShapeDtypeStruct((ntm, LANES), F32)),
        grid=(2, s // tm),
        in_specs=[pl.BlockSpec((tm, LANES), lambda p, i: (i, 0))],
        out_specs=(pl.BlockSpec((tm, LANES), lambda p, i: (i * p, 0)),
                   pl.BlockSpec((ntm, LANES), lambda p, i: (0, 0))),
        scratch_shapes=[pltpu.VMEM((1, LANES), F32), pltpu.VMEM((1, LANES), F32)],
        compiler_params=pltpu.CompilerParams(dimension_semantics=("arbitrary", "arbitrary"),
                                             vmem_limit_bytes=VMEM_LIMIT),
        name="plan",
    )(rinfo)


def _dispatch_kernel(pos_ref, texp_ref, nact_ref, h2_ref, xs_ref, zero_scr, sem, zsem, *, tm):
    i = pl.program_id(0)

    @pl.when(i == 0)
    def _():
        zero_scr[...] = jnp.zeros_like(zero_scr)
        nact = nact_ref[0]
        ntm = xs_ref.shape[0] // MOE_TILE

        def is_last(t):
            nxt = texp_ref[jnp.minimum(t + 1, nact - 1)]
            return (t >= nact - 1) | (nxt != texp_ref[jnp.minimum(t, nact - 1)])

        def zstart(t, carry):
            @pl.when(is_last(t))
            def _():
                pltpu.make_async_copy(zero_scr, xs_ref.at[pl.ds(pl.multiple_of(t * MOE_TILE, MOE_TILE), MOE_TILE), :],
                                      zsem).start()
            return carry

        def zwait(t, carry):
            @pl.when(is_last(t))
            def _():
                pltpu.make_async_copy(zero_scr, xs_ref.at[pl.ds(0, MOE_TILE), :], zsem).wait()
            return carry

        lax.fori_loop(0, ntm, zstart, 0)
        lax.fori_loop(0, ntm, zwait, 0)

    def start(r, carry):
        tok = i * tm + r
        for slot in range(2):
            p = pos_ref[2 * tok + slot]
            pltpu.make_async_copy(h2_ref.at[pl.ds(r, 1), :], xs_ref.at[pl.ds(p, 1), :], sem).start()
        return carry

    lax.fori_loop(0, tm, start, 0, unroll=8)
    for slot in range(2):
        pltpu.make_async_copy(h2_ref, xs_ref.at[pl.ds(0, tm), :], sem).wait()


def _dispatch(pos_flat, texp, nact, h2, tm=512):
    s, d = h2.shape
    ntm = _moe_num_tiles(s)
    return pl.pallas_call(
        functools.partial(_dispatch_kernel, tm=tm),
        out_shape=jax.ShapeDtypeStruct((ntm * MOE_TILE, d), F32),
        grid_spec=pltpu.PrefetchScalarGridSpec(
            num_scalar_prefetch=3,
            grid=(s // tm,),
            in_specs=[pl.BlockSpec((tm, d), lambda i, *_: (i, 0))],
            out_specs=pl.BlockSpec(memory_space=pl.ANY),
            scratch_shapes=[pltpu.VMEM((MOE_TILE, d), F32), pltpu.SemaphoreType.DMA, pltpu.SemaphoreType.DMA]),
        compiler_params=pltpu.CompilerParams(dimension_semantics=("arbitrary",),
                                             vmem_limit_bytes=VMEM_LIMIT),
        name="dispatch",
    )(pos_flat, texp, nact, h2)


def _experts_kernel(texp_ref, nact_ref, xs_ref, wg_hbm, wu_hbm, wd_hbm, ys_ref,
                    wg_f32, wu_f32, wd_f32, wg_bf, wu_bf, wd_bf, ord_scr, sem):
    i = pl.program_id(0)
    nact = nact_ref[0]
    j = jnp.minimum(i, nact - 1)
    e = texp_ref[j]
    fresh = (i == 0) | (e != texp_ref[jnp.maximum(j - 1, 0)])

    def weight_copies(expert, slot):
        return (pltpu.make_async_copy(wg_hbm.at[expert], wg_f32.at[slot], sem.at[slot]),
                pltpu.make_async_copy(wu_hbm.at[expert], wu_f32.at[slot], sem.at[slot]),
                pltpu.make_async_copy(wd_hbm.at[expert], wd_f32.at[slot], sem.at[slot]))

    @pl.when(i == 0)
    def _():
        ord_scr[0] = 0
        for c in weight_copies(e, 0):
            c.start()

    @pl.when((i < nact) & fresh)
    def _():
        slot = ord_scr[0] % 2
        for c in weight_copies(e, slot):
            c.wait()
        wg_bf[...] = wg_f32[slot].astype(BF16)
        wu_bf[...] = wu_f32[slot].astype(BF16)
        wd_bf[...] = wd_f32[slot].astype(BF16)
        nxt = lax.while_loop(lambda t: (t < nact) & (texp_ref[jnp.minimum(t, nact - 1)] == e),
                             lambda t: t + 1, i + 1)

        @pl.when(nxt < nact)
        def _():
            for c in weight_copies(texp_ref[nxt], 1 - slot):
                c.start()
        ord_scr[0] = ord_scr[0] + 1

    @pl.when(i < nact)
    def _():
        x = xs_ref[...].astype(BF16)
        hg = jnp.dot(x, wg_bf[...], preferred_element_type=F32)
        hu = jnp.dot(x, wu_bf[...], preferred_element_type=F32)
        a = (hg * _sigmoid(hg)) * hu
        ys_ref[...] = jnp.dot(a.astype(BF16), wd_bf[...], preferred_element_type=F32)

    @pl.when(i >= nact)
    def _():
        ys_ref[...] = jnp.zeros_like(ys_ref)


def _experts(texp, nact, xs, w_gate, w_up, w_down):
    p, d = xs.shape
    ntm = p // MOE_TILE
    _, _, f = w_gate.shape

    def tile(i, texp_ref, nact_ref):
        return (jnp.minimum(i, nact_ref[0] - 1), 0)

    hbm = pl.BlockSpec(memory_space=pl.ANY)
    return pl.pallas_call(
        _experts_kernel,
        out_shape=jax.ShapeDtypeStruct((p, d), F32),
        grid_spec=pltpu.PrefetchScalarGridSpec(
            num_scalar_prefetch=2,
            grid=(ntm,),
            in_specs=[pl.BlockSpec((MOE_TILE, d), tile), hbm, hbm, hbm],
            out_specs=pl.BlockSpec((MOE_TILE, d), lambda i, *_: (i, 0)),
            scratch_shapes=[pltpu.VMEM((2, d, f), F32), pltpu.VMEM((2, d, f), F32), pltpu.VMEM((2, f, d), F32),
                            pltpu.VMEM((d, f), BF16), pltpu.VMEM((d, f), BF16), pltpu.VMEM((f, d), BF16),
                            pltpu.SMEM((1,), jnp.int32), pltpu.SemaphoreType.DMA((2,))]),
        compiler_params=pltpu.CompilerParams(dimension_semantics=("arbitrary",),
                                             vmem_limit_bytes=VMEM_LIMIT),
        name="experts",
    )(texp, nact, xs, w_gate, w_up, w_down)


def _combine_kernel(pos_ref, ys_ref, rinfo_ref, x1_ref, gpost_ref, gt_ref, o_ref, ybuf, sem, *, tm):
    i = pl.program_id(0)

    def gather(tile, b):
        def start(r, carry):
            tok = tile * tm + r
            for slot in range(2):
                p = pos_ref[2 * tok + slot]
                pltpu.make_async_copy(ys_ref.at[pl.ds(p, 1), :], ybuf.at[b, slot, pl.ds(r, 1), :],
                                      sem.at[b]).start()
            return carry
        lax.fori_loop(0, tm, start, 0, unroll=8)

    @pl.when(i == 0)
    def _():
        gather(0, 0)

    @pl.when(i + 1 < pl.num_programs(0))
    def _():
        gather(i + 1, (i + 1) % 2)

    b = i % 2
    for slot in range(2):
        pltpu.make_async_copy(ys_ref.at[pl.ds(0, tm), :], ybuf.at[b, slot], sem.at[b]).wait()
    r = rinfo_ref[...]
    y = r[:, 2:3] * ybuf[b, 0] + r[:, 3:4] * ybuf[b, 1]
    o_ref[...] = x1_ref[...] + gt_ref[...] * (_rms(y, NORM_EPS) * gpost_ref[...])


def _combine(pos_flat, ys, rinfo, x1, gpost, gt, tm=512):
    s, d = x1.shape
    row = pl.BlockSpec((1, d), lambda i, *_: (0, 0))
    return pl.pallas_call(
        functools.partial(_combine_kernel, tm=tm),
        out_shape=jax.ShapeDtypeStruct((s, d), F32),
        grid_spec=pltpu.PrefetchScalarGridSpec(
            num_scalar_prefetch=1,
            grid=(s // tm,),
            in_specs=[pl.BlockSpec(memory_space=pl.ANY),
                      pl.BlockSpec((tm, LANES), lambda i, *_: (i, 0)),
                      pl.BlockSpec((tm, d), lambda i, *_: (i, 0)),
                      row, row],
            out_specs=pl.BlockSpec((tm, d), lambda i, *_: (i, 0)),
            scratch_shapes=[pltpu.VMEM((2, 2, tm, d), F32), pltpu.SemaphoreType.DMA((2,))]),
        compiler_params=pltpu.CompilerParams(dimension_semantics=("arbitrary",),
                                             vmem_limit_bytes=VMEM_LIMIT),
        name="combine",
    )(pos_flat, ys, rinfo, x1, gpost, gt)


def _moe(h2, rinfo, w_gate, w_up, w_down, x1, gpost, gt):
    pos, tinfo = _plan(rinfo)
    pos_flat = pos[:, 0:2].astype(jnp.int32).reshape(-1)
    texp = tinfo[:, 0].astype(jnp.int32)
    nact = tinfo[0:1, 1].astype(jnp.int32)
    xs = _dispatch(pos_flat, texp, nact, h2)
    ys = _experts(texp, nact, xs, w_gate, w_up, w_down)
    return _combine(pos_flat, ys, rinfo, x1, gpost, gt)


def kernel(x, c, w_ada, b_ada, g_pre1, w_in, lam_q1, lam_k1, lam_q2, lam_k2, g_subln, w_conv, w_out,
           g_post1, g_pre2, w_router_g, b_router_g, w_router_e, b_router_e, w_gate, w_up, w_down, g_post2):
    b, s, d = x.shape
    assert b == 1
    depth = w_ada.shape[0]
    slopes = jnp.asarray([2.0 ** (-8.0 * (i + 1) / N_HEADS) for i in range(N_HEADS)], F32)
    x2 = x.reshape(s, d)
    for l in range(depth):
        lam_init = 0.8 - 0.6 * math.exp(-0.3 * l)
        mod = _ada(c.reshape(d, 1), w_ada[l], b_ada[l].reshape(1, -1))
        sh1, sc1, gt1, sh2, sc2, gt2 = [mod[:, k * d:(k + 1) * d] for k in range(6)]
        proj = _inproj(x2, g_pre1[l].reshape(1, d), sc1, sh1, w_in[l])
        y_attn = _attn(slopes, lam_q1[l].reshape(1, -1), lam_k1[l].reshape(1, -1),
                       lam_q2[l].reshape(1, -1), lam_k2[l].reshape(1, -1), proj,
                       g_subln[l].reshape(1, -1), lam_init)
        wr = jnp.concatenate([w_router_g[l], w_router_e[l].reshape(d, N_EXPERTS),
                              jnp.zeros((d, LANES - N_GROUPS - N_EXPERTS), F32)], axis=1).astype(BF16)
        br = jnp.concatenate([b_router_g[l], b_router_e[l].reshape(N_EXPERTS),
                              jnp.zeros((LANES - N_GROUPS - N_EXPERTS,), F32)]).reshape(1, LANES)
        x1, h2, rinfo = _outproj(y_attn, proj, x2, w_conv[l], w_out[l].astype(BF16),
                                 g_post1[l].reshape(1, d), gt1, g_pre2[l].reshape(1, d), sc2, sh2, wr, br)
        x2 = _moe(h2, rinfo, w_gate[l], w_up[l], w_down[l], x1, g_post2[l].reshape(1, d), gt2)
    return x2.reshape(b, s, d)
```

```python
import functools
import math

import jax
import jax.numpy as jnp
from jax import lax
from jax.experimental import pallas as pl
from jax.experimental.pallas import tpu as pltpu

F32 = jnp.float32
BF16 = jnp.bfloat16

LANES = 128
CHUNK = 64
N_HEADS = 8
QK_DIM = 64
V_DIM = 128
CONV_WIDTH = 3
N_GROUPS = 4
EXPERTS_PER_GROUP = 8
N_EXPERTS = N_GROUPS * EXPERTS_PER_GROUP
NORM_EPS = 1e-6
SUBLN_EPS = 1e-5
NEG_BIG = -1e30
LOG2E = 1.4426950408889634
VMEM_LIMIT = 56 * 1024 * 1024


def _rms(x, eps):
    return x * lax.rsqrt(jnp.mean(x * x, axis=-1, keepdims=True) + eps)


def _sigmoid(x):
    return 1.0 / (1.0 + jnp.exp(-x))


def _ada_kernel(c_ref, w_ref, b_ref, o_ref):
    c = c_ref[...]
    s = c * _sigmoid(c)
    o_ref[...] = jnp.sum(w_ref[...] * s, axis=0, keepdims=True) + b_ref[...]


def _ada(c_col, w_ada, b_ada, tn=1024):
    d, n = w_ada.shape
    return pl.pallas_call(
        _ada_kernel,
        out_shape=jax.ShapeDtypeStruct((1, n), F32),
        grid=(n // tn,),
        in_specs=[pl.BlockSpec((d, 1), lambda j: (0, 0)),
                  pl.BlockSpec((d, tn), lambda j: (0, j)),
                  pl.BlockSpec((1, tn), lambda j: (0, j))],
        out_specs=pl.BlockSpec((1, tn), lambda j: (0, j)),
        compiler_params=pltpu.CompilerParams(dimension_semantics=("arbitrary",),
                                             vmem_limit_bytes=VMEM_LIMIT),
        name="ada",
    )(c_col, w_ada, b_ada)


def _inproj_kernel(x_ref, g_ref, sc_ref, sh_ref, w_ref, o_ref, h_scr, *, cn):
    @pl.when(pl.program_id(1) == 0)
    def _():
        rows = 16

        scale = g_ref[...] * (1.0 + sc_ref[...])

        def norm_rows(r, carry):
            sl = pl.ds(pl.multiple_of(r * rows, rows), rows)
            h_scr[sl, :] = (_rms(x_ref[sl, :], NORM_EPS) * scale + sh_ref[...]).astype(BF16)
            return carry
        lax.fori_loop(0, x_ref.shape[0] // rows, norm_rows, 0, unroll=8)

    acc = jnp.dot(h_scr[...], w_ref[...].astype(BF16), preferred_element_type=F32)
    for c in range(cn):
        o_ref[c] = acc[:, c * LANES:(c + 1) * LANES].astype(BF16)


def _inproj(x2, g, sc, sh, w, tm=1024, cn=4):
    s, d = x2.shape
    n = w.shape[1]
    nchunks = n // LANES
    row = lambda i, j: (0, 0)
    return pl.pallas_call(
        functools.partial(_inproj_kernel, cn=cn),
        out_shape=jax.ShapeDtypeStruct((nchunks, s, LANES), BF16),
        grid=(s // tm, nchunks // cn),
        in_specs=[pl.BlockSpec((tm, d), lambda i, j: (i, 0)),
                  pl.BlockSpec((1, d), row),
                  pl.BlockSpec((1, d), row),
                  pl.BlockSpec((1, d), row),
                  pl.BlockSpec((d, cn * LANES), lambda i, j: (0, j))],
        out_specs=pl.BlockSpec((cn, tm, LANES), lambda i, j: (j, i, 0)),
        scratch_shapes=[pltpu.VMEM((tm, d), BF16)],
        compiler_params=pltpu.CompilerParams(dimension_semantics=("arbitrary", "arbitrary"),
                                             vmem_limit_bytes=VMEM_LIMIT),
        name="inproj",
    )(x2, g, sc, sh, w)


ATT_POS = 256
ATT_VROWS = 144


def _attn_kernel(slope_ref, lq1_ref, lk1_ref, lq2_ref, lk2_ref, q_ref, k_ref, v_ref, gsub_ref,
                 o_ref, vt_scr, sa_scr, sb_scr, m_scr, acc_scr, corr_scr, *, t, lam_init):
    h = pl.program_id(0)
    nkt = vt_scr.shape[0]
    slope2 = slope_ref[h] * LOG2E
    c_hi = jnp.full((LANES, t), slope2, F32).astype(BF16).astype(F32)
    c_lo = (jnp.full((LANES, t), slope2, F32) - c_hi).astype(BF16).astype(F32)

    ones_row = jnp.where(lax.broadcasted_iota(jnp.int32, (ATT_VROWS - V_DIM, t), 0) == 0, 1.0, 0.0)

    def tr(c, carry):
        vc = v_ref[0, pl.ds(pl.multiple_of(c * t, t), t), :]
        vt_scr[c] = jnp.concatenate([vc.astype(F32).T, ones_row], axis=0).astype(BF16)
        return carry
    lax.fori_loop(0, nkt, tr, 0)

    krel = lax.broadcasted_iota(jnp.int32, (t, t), 0)
    qrel = lax.broadcasted_iota(jnp.int32, (t, t), 1)
    shift = CHUNK.bit_length() - 1
    allowed = (krel >> shift) <= (qrel >> shift)
    c_mxu = (c_hi + c_lo)[0:1, 0:1]
    corr_scr[...] = jnp.where(allowed, slope2 * (qrel - jnp.abs(qrel - krel)).astype(F32)
                              - c_mxu * krel.astype(F32), NEG_BIG)

    step_dec = slope2 * t
    row = lax.broadcasted_iota(jnp.int32, (LANES, t), 0)
    qfeat = jnp.where(row == 0, c_hi, jnp.where(row == 1, ATT_POS * c_hi,
                      jnp.where(row == 2, c_lo, jnp.where(row == 3, ATT_POS * c_lo, 0.0))))
    lam = (jnp.exp(jnp.sum(lq1_ref[...] * lk1_ref[...], axis=-1, keepdims=True))
           - jnp.exp(jnp.sum(lq2_ref[...] * lk2_ref[...], axis=-1, keepdims=True)) + lam_init)

    def colmax(s):
        while s.shape[0] > 8:
            half = s.shape[0] // 2
            s = jnp.maximum(s[:half], s[half:])
        return jnp.max(s, axis=0, keepdims=True)

    def query_tile(qi, carry):
        rows = pl.ds(pl.multiple_of(qi * t, t), t)
        qt = (q_ref[0, rows, :].astype(F32) * (LOG2E * QK_DIM ** -0.5)).T
        qa = jnp.concatenate([jnp.where(row < QK_DIM, qt, 0.0), qfeat], axis=0)
        qb = jnp.concatenate([jnp.where(row >= QK_DIM, qt, 0.0), qfeat], axis=0)
        qts = jnp.concatenate([qa, qb], axis=1).astype(BF16)

        krow = lax.broadcasted_iota(jnp.int32, (t, LANES), 0)
        klane = lax.broadcasted_iota(jnp.int32, (t, LANES), 1)
        pbits = ATT_POS.bit_length() - 1
        kmod = (krow & (ATT_POS - 1)).astype(F32)
        kdiv = (krow >> pbits).astype(F32)
        kfeat = jnp.where((klane == 0) | (klane == 2), kmod,
                          jnp.where((klane == 1) | (klane == 3), kdiv, 0.0)).astype(BF16)

        m_scr[...] = jnp.full(m_scr.shape, NEG_BIG, F32)
        acc_scr[...] = jnp.zeros(acc_scr.shape, F32)

        def scores(j, s_ref):
            kj = k_ref[0, pl.ds(pl.multiple_of(j * t, t), t), :]
            s_ref[...] = jnp.dot(jnp.concatenate([kj, kfeat], axis=1), qts, preferred_element_type=F32)

        def softmax_pv(j, s_ref, corr):
            s = s_ref[...]
            if corr is not None:
                s = s + corr
            m_old = m_scr[...] - step_dec
            m_new = jnp.maximum(m_old, colmax(s))
            alpha = jnp.exp2(m_old - m_new)
            p = jnp.exp2(s - m_new).astype(BF16)
            pv = jnp.dot(vt_scr[j], p, preferred_element_type=F32)
            acc_scr[...] = alpha * acc_scr[...] + pv
            m_scr[...] = m_new

        def near(s_ref):
            corr = corr_scr[...]
            softmax_pv(qi, s_ref, jnp.concatenate([corr, corr], axis=1))

        scores(0, sa_scr)

        def pair(i, c):
            scores(2 * i + 1, sb_scr)
            softmax_pv(2 * i, sa_scr, None)
            scores(2 * i + 2, sa_scr)
            softmax_pv(2 * i + 1, sb_scr, None)
            return c

        def quad(i, c):
            return pair(2 * i + 1, pair(2 * i, c))

        def octo(i, c):
            return quad(2 * i + 1, quad(2 * i, c))

        lax.fori_loop(0, qi // 8, octo, 0)
        lax.fori_loop(2 * (qi // 8), qi // 4, quad, 0)
        lax.fori_loop(2 * (qi // 4), qi // 2, pair, 0)

        @pl.when(qi % 2 == 1)
        def _():
            scores(qi, sb_scr)
            softmax_pv(qi - 1, sa_scr, None)
            near(sb_scr)

        @pl.when(qi % 2 == 0)
        def _():
            near(sa_scr)

        o1 = acc_scr[0:V_DIM, 0:t] / acc_scr[V_DIM:V_DIM + 1, 0:t]
        o2 = acc_scr[0:V_DIM, t:2 * t] / acc_scr[V_DIM:V_DIM + 1, t:2 * t]
        ot = o1 - lam * o2
        ot = ot * lax.rsqrt(jnp.mean(ot * ot, axis=0, keepdims=True) + SUBLN_EPS)
        o_ref[0, rows, :] = (ot.T * gsub_ref[...] * (1.0 - lam_init)).astype(BF16)
        return carry

    lax.fori_loop(0, nkt, query_tile, 0)


def _attn(slopes, lq1, lk1, lq2, lk2, proj, gsub, lam_init, t=512):
    s = proj.shape[1]
    smem = pl.BlockSpec(memory_space=pltpu.SMEM)
    vec = lambda n: pl.BlockSpec((1, n), lambda h: (0, 0))
    head = lambda base: pl.BlockSpec((1, s, LANES), lambda h, base=base: (base + h, 0, 0))
    return pl.pallas_call(
        functools.partial(_attn_kernel, t=t, lam_init=lam_init),
        out_shape=jax.ShapeDtypeStruct((N_HEADS, s, V_DIM), BF16),
        grid=(N_HEADS,),
        in_specs=[smem, vec(QK_DIM), vec(QK_DIM), vec(QK_DIM), vec(QK_DIM),
                  head(0), head(N_HEADS), head(2 * N_HEADS), vec(V_DIM)],
        out_specs=pl.BlockSpec((1, s, V_DIM), lambda h: (h, 0, 0)),
        scratch_shapes=[pltpu.VMEM((s // t, ATT_VROWS, t), BF16),
                        pltpu.VMEM((t, 2 * t), F32), pltpu.VMEM((t, 2 * t), F32),
                        pltpu.VMEM((1, 2 * t), F32),
                        pltpu.VMEM((ATT_VROWS, 2 * t), F32),
                        pltpu.VMEM((t, t), F32)],
        compiler_params=pltpu.CompilerParams(dimension_semantics=("arbitrary",),
                                             vmem_limit_bytes=VMEM_LIMIT),
        name="attn",
    )(slopes, lq1, lk1, lq2, lk2, proj, proj, proj, gsub)


def _route(logits):
    lane = lax.broadcasted_iota(jnp.int32, logits.shape, 1).astype(F32)
    far = float(4 * LANES)
    gl = jnp.where(lane < N_GROUPS, logits, NEG_BIG)
    gmax = jnp.max(gl, axis=-1, keepdims=True)
    g_w = 1.0 / jnp.sum(jnp.exp(gl - gmax), axis=-1, keepdims=True)
    g_idx = jnp.min(jnp.where(gl == gmax, lane, far), axis=-1, keepdims=True)
    lo = N_GROUPS + EXPERTS_PER_GROUP * g_idx
    el = jnp.where((lane >= lo) & (lane < lo + EXPERTS_PER_GROUP), logits, NEG_BIG)
    e1 = jnp.max(el, axis=-1, keepdims=True)
    i1 = jnp.min(jnp.where(el == e1, lane, far), axis=-1, keepdims=True)
    el2 = jnp.where(lane == i1, NEG_BIG, el)
    e2 = jnp.max(el2, axis=-1, keepdims=True)
    i2 = jnp.min(jnp.where(el2 == e2, lane, far), axis=-1, keepdims=True)
    r = jnp.exp(e2 - e1)
    w1 = 1.0 / (1.0 + r)
    w2 = r * w1
    out = jnp.where(lane == 0.0, i1 - N_GROUPS, 0.0)
    out = jnp.where(lane == 1.0, i2 - N_GROUPS, out)
    out = jnp.where(lane == 2.0, g_w * w1, out)
    out = jnp.where(lane == 3.0, g_w * w2, out)
    return out


def _outproj_kernel(ya_ref, b_ref, c_ref, u_ref, x_ref, wconv_ref, wout_ref, gpost_ref, gt_ref,
                    gpre_ref, sc_ref, sh_ref, wr_ref, br_ref,
                    x1_ref, h2_ref, rinfo_ref, zbuf, ybuf, *, tm, nc):
    i = pl.program_id(0)
    halo = 8

    @pl.when(i == 0)
    def _():
        zbuf[0:halo, :] = jnp.zeros((halo, zbuf.shape[1]), F32)

    @pl.when(i > 0)
    def _():
        zbuf[0:halo, :] = zbuf[tm:tm + halo, :]

    for c in range(nc):
        cols = slice(c * LANES, (c + 1) * LANES)
        zbuf[halo:halo + tm, cols] = c_ref[c].astype(F32) * u_ref[c].astype(F32)
    d_attn = ya_ref.shape[0] * LANES
    nh = 2
    hm = tm // nh
    for part in range(nh):
        r0 = part * hm
        rows = slice(r0, r0 + hm)
        for c in range(ya_ref.shape[0]):
            ybuf[rows, c * LANES:(c + 1) * LANES] = ya_ref[c, rows, :]
        for c in range(nc):
            cols = slice(c * LANES, (c + 1) * LANES)
            conv = (wconv_ref[0:1, cols] * zbuf[r0 + halo - 2:r0 + halo - 2 + hm, cols]
                    + wconv_ref[1:2, cols] * zbuf[r0 + halo - 1:r0 + halo - 1 + hm, cols]
                    + wconv_ref[2:3, cols] * zbuf[r0 + halo:r0 + halo + hm, cols])
            ybuf[rows, d_attn + c * LANES:d_attn + (c + 1) * LANES] = (
                b_ref[c, rows, :].astype(F32) * conv).astype(BF16)

        y = jnp.dot(ybuf[rows, :], wout_ref[...], preferred_element_type=F32)
        x1 = x_ref[rows, :] + gt_ref[...] * (_rms(y, NORM_EPS) * gpost_ref[...])
        x1_ref[rows, :] = x1
        h2 = (_rms(x1, NORM_EPS) * gpre_ref[...]) * (1.0 + sc_ref[...]) + sh_ref[...]
        h2_ref[rows, :] = h2
        logits = jnp.dot(h2.astype(BF16), wr_ref[...], preferred_element_type=F32) + br_ref[...]
        rinfo_ref[rows, :] = _route(logits)


def _outproj(y_attn, proj, x2, w_conv, wout_bf, gpost, gt, gpre, sc, sh, wr_bf, br, tm=512):
    s, d = x2.shape
    nh = y_attn.shape[0]
    d_attn = nh * LANES
    nc = (d - d_attn) // LANES
    base = 3 * d_attn // LANES
    row = lambda n: pl.BlockSpec((1, n), lambda i: (0, 0))
    blk = lambda g: pl.BlockSpec((nc, tm, LANES), lambda i, g=g: (base // nc + g, i, 0))
    return pl.pallas_call(
        functools.partial(_outproj_kernel, tm=tm, nc=nc),
        out_shape=(jax.ShapeDtypeStruct((s, d), F32),
                   jax.ShapeDtypeStruct((s, d), F32),
                   jax.ShapeDtypeStruct((s, LANES), F32)),
        grid=(s // tm,),
        in_specs=[pl.BlockSpec((nh, tm, LANES), lambda i: (0, i, 0)),
                  blk(0), blk(1), blk(2),
                  pl.BlockSpec((tm, d), lambda i: (i, 0)),
                  pl.BlockSpec((CONV_WIDTH, d - d_attn), lambda i: (0, 0)),
                  pl.BlockSpec((d, d), lambda i: (0, 0)),
                  row(d), row(d), row(d), row(d), row(d),
                  pl.BlockSpec((d, LANES), lambda i: (0, 0)),
                  row(LANES)],
        out_specs=(pl.BlockSpec((tm, d), lambda i: (i, 0)),
                   pl.BlockSpec((tm, d), lambda i: (i, 0)),
                   pl.BlockSpec((tm, LANES), lambda i: (i, 0))),
        scratch_shapes=[pltpu.VMEM((tm + 8, d - d_attn), F32), pltpu.VMEM((tm, d), BF16)],
        compiler_params=pltpu.CompilerParams(dimension_semantics=("arbitrary",),
                                             vmem_limit_bytes=VMEM_LIMIT),
        name="outproj",
    )(y_attn, proj, proj, proj, x2, w_conv, wout_bf, gpost, gt, gpre, sc, sh, wr_bf, br)


MOE_TILE = 256


def _moe_num_tiles(s):
    return (2 * s) // MOE_TILE + N_EXPERTS


def _plan_kernel(rinfo_ref, pos_ref, tinfo_ref, cnt_scr, base_scr, *, tm, ntm):
    ph = pl.program_id(0)
    i = pl.program_id(1)
    r = rinfo_ref[...]
    lane = lax.broadcasted_iota(jnp.int32, r.shape, 1).astype(F32)
    oh0 = jnp.where(lane == r[:, 0:1], 1.0, 0.0)
    oh1 = jnp.where(lane == r[:, 1:2], 1.0, 0.0)
    both = oh0 + oh1
    tile_cnt = jnp.sum(both, axis=0, keepdims=True)

    @pl.when((ph == 0) & (i == 0))
    def _():
        cnt_scr[...] = jnp.zeros_like(cnt_scr)

    @pl.when(ph == 0)
    def _():
        cnt_scr[...] += tile_cnt

    @pl.when((ph == 1) & (i == 0))
    def _():
        ktiles = jnp.floor((cnt_scr[...] + (MOE_TILE - 1)) * (1.0 / MOE_TILE))
        rr = lax.broadcasted_iota(jnp.int32, (LANES, LANES), 0)
        cc = lax.broadcasted_iota(jnp.int32, (LANES, LANES), 1)
        upper = jnp.where(rr < cc, 1.0, 0.0).astype(BF16)
        first = jnp.dot(jnp.broadcast_to(ktiles, (8, LANES)).astype(BF16), upper,
                        preferred_element_type=F32)[0:1, :]
        base_scr[...] = first * MOE_TILE
        ti = lax.broadcasted_iota(jnp.int32, (ntm, LANES), 0).astype(F32)
        el = lax.broadcasted_iota(jnp.int32, (ntm, LANES), 1).astype(F32)
        owned = jnp.where(ti >= first, jnp.where(ti < first + ktiles, 1.0, 0.0), 0.0)
        texp = jnp.sum(owned * el, axis=1, keepdims=True)
        nact = jnp.sum(ktiles, axis=1, keepdims=True)
        tinfo_ref[...] = jnp.where(el == 0.0, texp, jnp.where(el == 1.0, nact, 0.0))

    @pl.when(ph == 1)
    def _():
        rr = lax.broadcasted_iota(jnp.int32, (tm, tm), 0)
        cc = lax.broadcasted_iota(jnp.int32, (tm, tm), 1)
        lower = jnp.where(cc < rr, 1.0, 0.0).astype(BF16)
        before = jnp.dot(lower, both.astype(BF16), preferred_element_type=F32)
        val = before + base_scr[...]
        p0 = jnp.sum(oh0 * val, axis=1, keepdims=True)
        p1 = jnp.sum(oh1 * val, axis=1, keepdims=True)
        pos_ref[...] = jnp.where(lane == 0.0, p0, jnp.where(lane == 1.0, p1, 0.0))
        base_scr[...] += tile_cnt


def _plan(rinfo, tm=512):
    s = rinfo.shape[0]
    ntm = _moe_num_tiles(s)
    return pl.pallas_call(
        functools.partial(_plan_kernel, tm=tm, ntm=ntm),
        out_shape=(jax.ShapeDtypeStruct((s, LANES), F32), jax.ShapeDtypeStruct((ntm, LANES), F32)),
        grid=(2, s // tm),
        in_specs=[pl.BlockSpec((tm, LANES), lambda p, i: (i, 0))],
        out_specs=(pl.BlockSpec((tm, LANES), lambda p, i: (i * p, 0)),
                   pl.BlockSpec((ntm, LANES), lambda p, i: (0, 0))),
        scratch_shapes=[pltpu.VMEM((1, LANES), F32), pltpu.VMEM((1, LANES), F32)],
        compiler_params=pltpu.CompilerParams(dimension_semantics=("arbitrary", "arbitrary"),
                                             vmem_limit_bytes=VMEM_LIMIT),
        name="plan",
    )(rinfo)


def _dispatch_kernel(pos_ref, texp_ref, nact_ref, h2_ref, xs_ref, zero_scr, sem, zsem, *, tm):
    i = pl.program_id(0)

    @pl.when(i == 0)
    def _():
        zero_scr[...] = jnp.zeros_like(zero_scr)
        nact = nact_ref[0]
        ntm = xs_ref.shape[0] // MOE_TILE

        def is_last(t):
            nxt = texp_ref[jnp.minimum(t + 1, nact - 1)]
            return (t >= nact - 1) | (nxt != texp_ref[jnp.minimum(t, nact - 1)])

        def zstart(t, carry):
            @pl.when(is_last(t))
            def _():
                pltpu.make_async_copy(zero_scr, xs_ref.at[pl.ds(pl.multiple_of(t * MOE_TILE, MOE_TILE), MOE_TILE), :],
                                      zsem).start()
            return carry

        def zwait(t, carry):
            @pl.when(is_last(t))
            def _():
                pltpu.make_async_copy(zero_scr, xs_ref.at[pl.ds(0, MOE_TILE), :], zsem).wait()
            return carry

        lax.fori_loop(0, ntm, zstart, 0)
        lax.fori_loop(0, ntm, zwait, 0)

    def start(r, carry):
        tok = i * tm + r
        for slot in range(2):
            p = pos_ref[2 * tok + slot]
            pltpu.make_async_copy(h2_ref.at[pl.ds(r, 1), :], xs_ref.at[pl.ds(p, 1), :], sem).start()
        return carry

    lax.fori_loop(0, tm, start, 0, unroll=8)
    for slot in range(2):
        pltpu.make_async_copy(h2_ref, xs_ref.at[pl.ds(0, tm), :], sem).wait()


def _dispatch(pos_flat, texp, nact, h2, tm=1024):
    s, d = h2.shape
    ntm = _moe_num_tiles(s)
    return pl.pallas_call(
        functools.partial(_dispatch_kernel, tm=tm),
        out_shape=jax.ShapeDtypeStruct((ntm * MOE_TILE, d), F32),
        grid_spec=pltpu.PrefetchScalarGridSpec(
            num_scalar_prefetch=3,
            grid=(s // tm,),
            in_specs=[pl.BlockSpec((tm, d), lambda i, *_: (i, 0))],
            out_specs=pl.BlockSpec(memory_space=pl.ANY),
            scratch_shapes=[pltpu.VMEM((MOE_TILE, d), F32), pltpu.SemaphoreType.DMA, pltpu.SemaphoreType.DMA]),
        compiler_params=pltpu.CompilerParams(dimension_semantics=("arbitrary",),
                                             vmem_limit_bytes=VMEM_LIMIT),
        name="dispatch",
    )(pos_flat, texp, nact, h2)


def _experts_kernel(texp_ref, nact_ref, xs_ref, wg_hbm, wu_hbm, wd_hbm, ys_ref,
                    wg_f32, wu_f32, wd_f32, ord_scr, sem):
    i = pl.program_id(0)
    nact = nact_ref[0]
    j = jnp.minimum(i, nact - 1)
    e = texp_ref[j]
    fresh = (i == 0) | (e != texp_ref[jnp.maximum(j - 1, 0)])

    def weight_copies(expert, slot):
        return (pltpu.make_async_copy(wg_hbm.at[expert], wg_f32.at[slot], sem.at[slot]),
                pltpu.make_async_copy(wu_hbm.at[expert], wu_f32.at[slot], sem.at[slot]),
                pltpu.make_async_copy(wd_hbm.at[expert], wd_f32.at[slot], sem.at[slot]))

    @pl.when(i == 0)
    def _():
        ord_scr[0] = 0
        for c in weight_copies(e, 0):
            c.start()

    @pl.when((i < nact) & fresh)
    def _():
        slot = ord_scr[0] % 2
        for c in weight_copies(e, slot):
            c.wait()
        nxt = lax.while_loop(lambda t: (t < nact) & (texp_ref[jnp.minimum(t, nact - 1)] == e),
                             lambda t: t + 1, i + 1)

        @pl.when(nxt < nact)
        def _():
            for c in weight_copies(texp_ref[nxt], 1 - slot):
                c.start(priority=1)
        ord_scr[0] = ord_scr[0] + 1

    @pl.when(i < nact)
    def _():
        slot = (ord_scr[0] + 1) % 2
        x = xs_ref[...].astype(BF16)
        hg = jnp.dot(x, wg_f32[slot].astype(BF16), preferred_element_type=F32)
        hu = jnp.dot(x, wu_f32[slot].astype(BF16), preferred_element_type=F32)
        a = (hg * _sigmoid(hg)) * hu
        ys_ref[...] = jnp.dot(a.astype(BF16), wd_f32[slot].astype(BF16), preferred_element_type=F32)

    @pl.when(i >= nact)
    def _():
        ys_ref[...] = jnp.zeros_like(ys_ref)


def _experts(texp, nact, xs, w_gate, w_up, w_down):
    p, d = xs.shape
    ntm = p // MOE_TILE
    _, _, f = w_gate.shape

    def tile(i, texp_ref, nact_ref):
        return (jnp.minimum(i, nact_ref[0] - 1), 0)

    hbm = pl.BlockSpec(memory_space=pl.ANY)
    return pl.pallas_call(
        _experts_kernel,
        out_shape=jax.ShapeDtypeStruct((p, d), F32),
        grid_spec=pltpu.PrefetchScalarGridSpec(
            num_scalar_prefetch=2,
            grid=(ntm,),
            in_specs=[pl.BlockSpec((MOE_TILE, d), tile), hbm, hbm, hbm],
            out_specs=pl.BlockSpec((MOE_TILE, d), lambda i, *_: (i, 0)),
            scratch_shapes=[pltpu.VMEM((2, d, f), F32), pltpu.VMEM((2, d, f), F32), pltpu.VMEM((2, f, d), F32),
                            pltpu.SMEM((1,), jnp.int32), pltpu.SemaphoreType.DMA((2,))]),
        compiler_params=pltpu.CompilerParams(dimension_semantics=("arbitrary",),
                                             vmem_limit_bytes=VMEM_LIMIT),
        name="experts",
    )(texp, nact, xs, w_gate, w_up, w_down)


def _combine_kernel(pos_ref, ys_ref, rinfo_ref, x1_ref, gpost_ref, gt_ref, o_ref, ybuf, sem, *, tm):
    i = pl.program_id(0)

    def gather(tile, b):
        def start(r, carry):
            tok = tile * tm + r
            for slot in range(2):
                p = pos_ref[2 * tok + slot]
                pltpu.make_async_copy(ys_ref.at[pl.ds(p, 1), :], ybuf.at[b, slot, pl.ds(r, 1), :],
                                      sem.at[b]).start()
            return carry
        lax.fori_loop(0, tm, start, 0, unroll=8)

    @pl.when(i == 0)
    def _():
        gather(0, 0)

    @pl.when(i + 1 < pl.num_programs(0))
    def _():
        gather(i + 1, (i + 1) % 2)

    b = i % 2
    for slot in range(2):
        pltpu.make_async_copy(ys_ref.at[pl.ds(0, tm), :], ybuf.at[b, slot], sem.at[b]).wait()
    r = rinfo_ref[...]
    y = r[:, 2:3] * ybuf[b, 0] + r[:, 3:4] * ybuf[b, 1]
    o_ref[...] = x1_ref[...] + gt_ref[...] * (_rms(y, NORM_EPS) * gpost_ref[...])


def _combine(pos_flat, ys, rinfo, x1, gpost, gt, tm=512):
    s, d = x1.shape
    row = pl.BlockSpec((1, d), lambda i, *_: (0, 0))
    return pl.pallas_call(
        functools.partial(_combine_kernel, tm=tm),
        out_shape=jax.ShapeDtypeStruct((s, d), F32),
        grid_spec=pltpu.PrefetchScalarGridSpec(
            num_scalar_prefetch=1,
            grid=(s // tm,),
            in_specs=[pl.BlockSpec(memory_space=pl.ANY),
                      pl.BlockSpec((tm, LANES), lambda i, *_: (i, 0)),
                      pl.BlockSpec((tm, d), lambda i, *_: (i, 0)),
                      row, row],
            out_specs=pl.BlockSpec((tm, d), lambda i, *_: (i, 0)),
            scratch_shapes=[pltpu.VMEM((2, 2, tm, d), F32), pltpu.SemaphoreType.DMA((2,))]),
        compiler_params=pltpu.CompilerParams(dimension_semantics=("arbitrary",),
                                             vmem_limit_bytes=VMEM_LIMIT),
        name="combine",
    )(pos_flat, ys, rinfo, x1, gpost, gt)


def _moe(h2, rinfo, w_gate, w_up, w_down, x1, gpost, gt):
    pos, tinfo = _plan(rinfo)
    pos_flat = pos[:, 0:2].astype(jnp.int32).reshape(-1)
    texp = tinfo[:, 0].astype(jnp.int32)
    nact = tinfo[0:1, 1].astype(jnp.int32)
    xs = _dispatch(pos_flat, texp, nact, h2)
    ys = _experts(texp, nact, xs, w_gate, w_up, w_down)
    return _combine(pos_flat, ys, rinfo, x1, gpost, gt)


def kernel(x, c, w_ada, b_ada, g_pre1, w_in, lam_q1, lam_k1, lam_q2, lam_k2, g_subln, w_conv, w_out,
           g_post1, g_pre2, w_router_g, b_router_g, w_router_e, b_router_e, w_gate, w_up, w_down, g_post2):
    b, s, d = x.shape
    assert b == 1
    depth = w_ada.shape[0]
    slopes = jnp.asarray([2.0 ** (-8.0 * (i + 1) / N_HEADS) for i in range(N_HEADS)], F32)
    x2 = x.reshape(s, d)
    for l in range(depth):
        lam_init = 0.8 - 0.6 * math.exp(-0.3 * l)
        mod = _ada(c.reshape(d, 1), w_ada[l], b_ada[l].reshape(1, -1))
        sh1, sc1, gt1, sh2, sc2, gt2 = [mod[:, k * d:(k + 1) * d] for k in range(6)]
        proj = _inproj(x2, g_pre1[l].reshape(1, d), sc1, sh1, w_in[l])
        y_attn = _attn(slopes, lam_q1[l].reshape(1, -1), lam_k1[l].reshape(1, -1),
                       lam_q2[l].reshape(1, -1), lam_k2[l].reshape(1, -1), proj,
                       g_subln[l].reshape(1, -1), lam_init)
        wr = jnp.concatenate([w_router_g[l], w_router_e[l].reshape(d, N_EXPERTS),
                              jnp.zeros((d, LANES - N_GROUPS - N_EXPERTS), F32)], axis=1).astype(BF16)
        br = jnp.concatenate([b_router_g[l], b_router_e[l].reshape(N_EXPERTS),
                              jnp.zeros((LANES - N_GROUPS - N_EXPERTS,), F32)]).reshape(1, LANES)
        x1, h2, rinfo = _outproj(y_attn, proj, x2, w_conv[l], w_out[l].astype(BF16),
                                 g_post1[l].reshape(1, d), gt1, g_pre2[l].reshape(1, d), sc2, sh2, wr, br)
        x2 = _moe(h2, rinfo, w_gate[l], w_up[l], w_down[l], x1, g_post2[l].reshape(1, d), gt2)
    return x2.reshape(b, s, d)
```

```python
import functools
import math

import jax
import jax.numpy as jnp
from jax import lax
from jax.experimental import pallas as pl
from jax.experimental.pallas import tpu as pltpu

F32 = jnp.float32
BF16 = jnp.bfloat16

LANES = 128
CHUNK = 64
N_HEADS = 8
QK_DIM = 64
V_DIM = 128
CONV_WIDTH = 3
N_GROUPS = 4
EXPERTS_PER_GROUP = 8
N_EXPERTS = N_GROUPS * EXPERTS_PER_GROUP
NORM_EPS = 1e-6
SUBLN_EPS = 1e-5
NEG_BIG = -1e30
LOG2E = 1.4426950408889634
VMEM_LIMIT = 56 * 1024 * 1024


def _rms(x, eps):
    return x * lax.rsqrt(jnp.mean(x * x, axis=-1, keepdims=True) + eps)


def _sigmoid(x):
    return 1.0 / (1.0 + jnp.exp(-x))


def _ada_kernel(c_ref, w_ref, b_ref, o_ref):
    c = c_ref[...]
    s = c * _sigmoid(c)
    o_ref[...] = jnp.sum(w_ref[...] * s, axis=0, keepdims=True) + b_ref[...]


def _ada(c_col, w_ada, b_ada, tn=1024):
    d, n = w_ada.shape
    return pl.pallas_call(
        _ada_kernel,
        out_shape=jax.ShapeDtypeStruct((1, n), F32),
        grid=(n // tn,),
        in_specs=[pl.BlockSpec((d, 1), lambda j: (0, 0)),
                  pl.BlockSpec((d, tn), lambda j: (0, j)),
                  pl.BlockSpec((1, tn), lambda j: (0, j))],
        out_specs=pl.BlockSpec((1, tn), lambda j: (0, j)),
        compiler_params=pltpu.CompilerParams(dimension_semantics=("arbitrary",),
                                             vmem_limit_bytes=VMEM_LIMIT),
        name="ada",
    )(c_col, w_ada, b_ada)


def _inproj_kernel(x_ref, g_ref, sc_ref, sh_ref, w_ref, o_ref, h_scr, *, cn):
    @pl.when(pl.program_id(1) == 0)
    def _():
        rows = 16

        scale = g_ref[...] * (1.0 + sc_ref[...])

        def norm_rows(r, carry):
            sl = pl.ds(pl.multiple_of(r * rows, rows), rows)
            h_scr[sl, :] = (_rms(x_ref[sl, :], NORM_EPS) * scale + sh_ref[...]).astype(BF16)
            return carry
        lax.fori_loop(0, x_ref.shape[0] // rows, norm_rows, 0, unroll=8)

    acc = jnp.dot(h_scr[...], w_ref[...].astype(BF16), preferred_element_type=F32)
    for c in range(cn):
        o_ref[c] = acc[:, c * LANES:(c + 1) * LANES].astype(BF16)


def _inproj(x2, g, sc, sh, w, tm=1024, cn=12):
    s, d = x2.shape
    n = w.shape[1]
    nchunks = n // LANES
    row = lambda i, j: (0, 0)
    return pl.pallas_call(
        functools.partial(_inproj_kernel, cn=cn),
        out_shape=jax.ShapeDtypeStruct((nchunks, s, LANES), BF16),
        grid=(s // tm, nchunks // cn),
        in_specs=[pl.BlockSpec((tm, d), lambda i, j: (i, 0)),
                  pl.BlockSpec((1, d), row),
                  pl.BlockSpec((1, d), row),
                  pl.BlockSpec((1, d), row),
                  pl.BlockSpec((d, cn * LANES), lambda i, j: (0, j))],
        out_specs=pl.BlockSpec((cn, tm, LANES), lambda i, j: (j, i, 0)),
        scratch_shapes=[pltpu.VMEM((tm, d), BF16)],
        compiler_params=pltpu.CompilerParams(dimension_semantics=("arbitrary", "arbitrary"),
                                             vmem_limit_bytes=VMEM_LIMIT),
        name="inproj",
    )(x2, g, sc, sh, w)


ATT_POS = 256
ATT_VROWS = 144


def _attn_kernel(slope_ref, lq1_ref, lk1_ref, lq2_ref, lk2_ref, q_ref, k_ref, v_ref, gsub_ref,
                 o_ref, vt_scr, sa_scr, sb_scr, m_scr, acc_scr, corr_scr, *, t, lam_init):
    h = pl.program_id(0)
    nkt = vt_scr.shape[0]
    slope2 = slope_ref[h] * LOG2E
    c_hi = jnp.full((LANES, t), slope2, F32).astype(BF16).astype(F32)
    c_lo = (jnp.full((LANES, t), slope2, F32) - c_hi).astype(BF16).astype(F32)

    ones_row = jnp.where(lax.broadcasted_iota(jnp.int32, (ATT_VROWS - V_DIM, t), 0) == 0, 1.0, 0.0)

    def tr(c, carry):
        vc = v_ref[0, pl.ds(pl.multiple_of(c * t, t), t), :]
        vt_scr[c] = jnp.concatenate([vc.astype(F32).T, ones_row], axis=0).astype(BF16)
        return carry
    lax.fori_loop(0, nkt, tr, 0)

    krel = lax.broadcasted_iota(jnp.int32, (t, t), 0)
    qrel = lax.broadcasted_iota(jnp.int32, (t, t), 1)
    shift = CHUNK.bit_length() - 1
    allowed = (krel >> shift) <= (qrel >> shift)
    c_mxu = (c_hi + c_lo)[0:1, 0:1]
    corr_scr[...] = jnp.where(allowed, slope2 * (qrel - jnp.abs(qrel - krel)).astype(F32)
                              - c_mxu * krel.astype(F32), NEG_BIG)

    step_dec = slope2 * t
    row = lax.broadcasted_iota(jnp.int32, (LANES, t), 0)
    qfeat = jnp.where(row == 0, c_hi, jnp.where(row == 1, ATT_POS * c_hi,
                      jnp.where(row == 2, c_lo, jnp.where(row == 3, ATT_POS * c_lo, 0.0))))
    lam = (jnp.exp(jnp.sum(lq1_ref[...] * lk1_ref[...], axis=-1, keepdims=True))
           - jnp.exp(jnp.sum(lq2_ref[...] * lk2_ref[...], axis=-1, keepdims=True)) + lam_init)

    def colmax(s):
        while s.shape[0] > 8:
            half = s.shape[0] // 2
            s = jnp.maximum(s[:half], s[half:])
        return jnp.max(s, axis=0, keepdims=True)

    def query_tile(qi, carry):
        rows = pl.ds(pl.multiple_of(qi * t, t), t)
        qt = (q_ref[0, rows, :].astype(F32) * (LOG2E * QK_DIM ** -0.5)).T
        qa = jnp.concatenate([jnp.where(row < QK_DIM, qt, 0.0), qfeat], axis=0)
        qb = jnp.concatenate([jnp.where(row >= QK_DIM, qt, 0.0), qfeat], axis=0)
        qts = jnp.concatenate([qa, qb], axis=1).astype(BF16)

        krow = lax.broadcasted_iota(jnp.int32, (t, LANES), 0)
        klane = lax.broadcasted_iota(jnp.int32, (t, LANES), 1)
        pbits = ATT_POS.bit_length() - 1
        kmod = (krow & (ATT_POS - 1)).astype(F32)
        kdiv = (krow >> pbits).astype(F32)
        kfeat = jnp.where((klane == 0) | (klane == 2), kmod,
                          jnp.where((klane == 1) | (klane == 3), kdiv, 0.0)).astype(BF16)

        m_scr[...] = jnp.full(m_scr.shape, NEG_BIG, F32)
        acc_scr[...] = jnp.zeros(acc_scr.shape, F32)

        def scores(j, s_ref):
            kj = k_ref[0, pl.ds(pl.multiple_of(j * t, t), t), :]
            s_ref[...] = jnp.dot(jnp.concatenate([kj, kfeat], axis=1), qts, preferred_element_type=F32)

        def softmax_pv(j, s_ref, corr):
            s = s_ref[...]
            if corr is not None:
                s = s + corr
            m_old = m_scr[...] - step_dec
            m_new = jnp.maximum(m_old, colmax(s))
            alpha = jnp.exp2(m_old - m_new)
            p = jnp.exp2(s - m_new).astype(BF16)
            pv = jnp.dot(vt_scr[j], p, preferred_element_type=F32)
            acc_scr[...] = alpha * acc_scr[...] + pv
            m_scr[...] = m_new

        def near(s_ref):
            corr = corr_scr[...]
            softmax_pv(qi, s_ref, jnp.concatenate([corr, corr], axis=1))

        scores(0, sa_scr)

        def pair(i, c):
            scores(2 * i + 1, sb_scr)
            softmax_pv(2 * i, sa_scr, None)
            scores(2 * i + 2, sa_scr)
            softmax_pv(2 * i + 1, sb_scr, None)
            return c

        def quad(i, c):
            return pair(2 * i + 1, pair(2 * i, c))

        def octo(i, c):
            return quad(2 * i + 1, quad(2 * i, c))

        lax.fori_loop(0, qi // 8, octo, 0)
        lax.fori_loop(2 * (qi // 8), qi // 4, quad, 0)
        lax.fori_loop(2 * (qi // 4), qi // 2, pair, 0)

        @pl.when(qi % 2 == 1)
        def _():
            scores(qi, sb_scr)
            softmax_pv(qi - 1, sa_scr, None)
            near(sb_scr)

        @pl.when(qi % 2 == 0)
        def _():
            near(sa_scr)

        o1 = acc_scr[0:V_DIM, 0:t] / acc_scr[V_DIM:V_DIM + 1, 0:t]
        o2 = acc_scr[0:V_DIM, t:2 * t] / acc_scr[V_DIM:V_DIM + 1, t:2 * t]
        ot = o1 - lam * o2
        ot = ot * lax.rsqrt(jnp.mean(ot * ot, axis=0, keepdims=True) + SUBLN_EPS)
        o_ref[0, rows, :] = (ot.T * gsub_ref[...] * (1.0 - lam_init)).astype(BF16)
        return carry

    lax.fori_loop(0, nkt, query_tile, 0)


def _attn(slopes, lq1, lk1, lq2, lk2, proj, gsub, lam_init, t=512):
    s = proj.shape[1]
    smem = pl.BlockSpec(memory_space=pltpu.SMEM)
    vec = lambda n: pl.BlockSpec((1, n), lambda h: (0, 0))
    head = lambda base: pl.BlockSpec((1, s, LANES), lambda h, base=base: (base + h, 0, 0))
    return pl.pallas_call(
        functools.partial(_attn_kernel, t=t, lam_init=lam_init),
        out_shape=jax.ShapeDtypeStruct((N_HEADS, s, V_DIM), BF16),
        grid=(N_HEADS,),
        in_specs=[smem, vec(QK_DIM), vec(QK_DIM), vec(QK_DIM), vec(QK_DIM),
                  head(0), head(N_HEADS), head(2 * N_HEADS), vec(V_DIM)],
        out_specs=pl.BlockSpec((1, s, V_DIM), lambda h: (h, 0, 0)),
        scratch_shapes=[pltpu.VMEM((s // t, ATT_VROWS, t), BF16),
                        pltpu.VMEM((t, 2 * t), F32), pltpu.VMEM((t, 2 * t), F32),
                        pltpu.VMEM((1, 2 * t), F32),
                        pltpu.VMEM((ATT_VROWS, 2 * t), F32),
                        pltpu.VMEM((t, t), F32)],
        compiler_params=pltpu.CompilerParams(dimension_semantics=("arbitrary",),
                                             vmem_limit_bytes=VMEM_LIMIT),
        name="attn",
    )(slopes, lq1, lk1, lq2, lk2, proj, proj, proj, gsub)


def _route(logits):
    lane = lax.broadcasted_iota(jnp.int32, logits.shape, 1).astype(F32)
    far = float(4 * LANES)
    gl = jnp.where(lane < N_GROUPS, logits, NEG_BIG)
    gmax = jnp.max(gl, axis=-1, keepdims=True)
    g_w = 1.0 / jnp.sum(jnp.exp(gl - gmax), axis=-1, keepdims=True)
    g_idx = jnp.min(jnp.where(gl == gmax, lane, far), axis=-1, keepdims=True)
    lo = N_GROUPS + EXPERTS_PER_GROUP * g_idx
    el = jnp.where((lane >= lo) & (lane < lo + EXPERTS_PER_GROUP), logits, NEG_BIG)
    e1 = jnp.max(el, axis=-1, keepdims=True)
    i1 = jnp.min(jnp.where(el == e1, lane, far), axis=-1, keepdims=True)
    el2 = jnp.where(lane == i1, NEG_BIG, el)
    e2 = jnp.max(el2, axis=-1, keepdims=True)
    i2 = jnp.min(jnp.where(el2 == e2, lane, far), axis=-1, keepdims=True)
    r = jnp.exp(e2 - e1)
    w1 = 1.0 / (1.0 + r)
    w2 = r * w1
    out = jnp.where(lane == 0.0, i1 - N_GROUPS, 0.0)
    out = jnp.where(lane == 1.0, i2 - N_GROUPS, out)
    out = jnp.where(lane == 2.0, g_w * w1, out)
    out = jnp.where(lane == 3.0, g_w * w2, out)
    return out


def _outproj_kernel(ya_ref, b_ref, c_ref, u_ref, x_ref, wconv_ref, wout_ref, gpost_ref, gt_ref,
                    gpre_ref, sc_ref, sh_ref, wr_ref, br_ref,
                    x1_ref, h2_ref, rinfo_ref, zbuf, ybuf, *, tm, nc):
    i = pl.program_id(0)
    halo = 8

    @pl.when(i == 0)
    def _():
        zbuf[0:halo, :] = jnp.zeros((halo, zbuf.shape[1]), F32)

    @pl.when(i > 0)
    def _():
        zbuf[0:halo, :] = zbuf[tm:tm + halo, :]

    for c in range(nc):
        cols = slice(c * LANES, (c + 1) * LANES)
        zbuf[halo:halo + tm, cols] = c_ref[c].astype(F32) * u_ref[c].astype(F32)
    d_attn = ya_ref.shape[0] * LANES
    nh = 2
    hm = tm // nh
    for part in range(nh):
        r0 = part * hm
        rows = slice(r0, r0 + hm)
        for c in range(ya_ref.shape[0]):
            ybuf[rows, c * LANES:(c + 1) * LANES] = ya_ref[c, rows, :]
        for c in range(nc):
            cols = slice(c * LANES, (c + 1) * LANES)
            conv = (wconv_ref[0:1, cols] * zbuf[r0 + halo - 2:r0 + halo - 2 + hm, cols]
                    + wconv_ref[1:2, cols] * zbuf[r0 + halo - 1:r0 + halo - 1 + hm, cols]
                    + wconv_ref[2:3, cols] * zbuf[r0 + halo:r0 + halo + hm, cols])
            ybuf[rows, d_attn + c * LANES:d_attn + (c + 1) * LANES] = (
                b_ref[c, rows, :].astype(F32) * conv).astype(BF16)

        y = jnp.dot(ybuf[rows, :], wout_ref[...], preferred_element_type=F32)
        x1 = x_ref[rows, :] + gt_ref[...] * (_rms(y, NORM_EPS) * gpost_ref[...])
        x1_ref[rows, :] = x1
        h2 = (_rms(x1, NORM_EPS) * gpre_ref[...]) * (1.0 + sc_ref[...]) + sh_ref[...]
        h2_ref[rows, :] = h2
        logits = jnp.dot(h2.astype(BF16), wr_ref[...], preferred_element_type=F32) + br_ref[...]
        rinfo_ref[rows, :] = _route(logits)


def _outproj(y_attn, proj, x2, w_conv, wout_bf, gpost, gt, gpre, sc, sh, wr_bf, br, tm=512):
    s, d = x2.shape
    nh = y_attn.shape[0]
    d_attn = nh * LANES
    nc = (d - d_attn) // LANES
    base = 3 * d_attn // LANES
    row = lambda n: pl.BlockSpec((1, n), lambda i: (0, 0))
    blk = lambda g: pl.BlockSpec((nc, tm, LANES), lambda i, g=g: (base // nc + g, i, 0))
    return pl.pallas_call(
        functools.partial(_outproj_kernel, tm=tm, nc=nc),
        out_shape=(jax.ShapeDtypeStruct((s, d), F32),
                   jax.ShapeDtypeStruct((s, d), F32),
                   jax.ShapeDtypeStruct((s, LANES), F32)),
        grid=(s // tm,),
        in_specs=[pl.BlockSpec((nh, tm, LANES), lambda i: (0, i, 0)),
                  blk(0), blk(1), blk(2),
                  pl.BlockSpec((tm, d), lambda i: (i, 0)),
                  pl.BlockSpec((CONV_WIDTH, d - d_attn), lambda i: (0, 0)),
                  pl.BlockSpec((d, d), lambda i: (0, 0)),
                  row(d), row(d), row(d), row(d), row(d),
                  pl.BlockSpec((d, LANES), lambda i: (0, 0)),
                  row(LANES)],
        out_specs=(pl.BlockSpec((tm, d), lambda i: (i, 0)),
                   pl.BlockSpec((tm, d), lambda i: (i, 0)),
                   pl.BlockSpec((tm, LANES), lambda i: (i, 0))),
        scratch_shapes=[pltpu.VMEM((tm + 8, d - d_attn), F32), pltpu.VMEM((tm, d), BF16)],
        compiler_params=pltpu.CompilerParams(dimension_semantics=("arbitrary",),
                                             vmem_limit_bytes=VMEM_LIMIT),
        name="outproj",
    )(y_attn, proj, proj, proj, x2, w_conv, wout_bf, gpost, gt, gpre, sc, sh, wr_bf, br)


MOE_TILE = 256


def _moe_num_tiles(s):
    return (2 * s) // MOE_TILE + N_EXPERTS


def _plan_kernel(rinfo_ref, pos_ref, tinfo_ref, cnt_scr, base_scr, *, tm, ntm):
    ph = pl.program_id(0)
    i = pl.program_id(1)
    r = rinfo_ref[...]
    lane = lax.broadcasted_iota(jnp.int32, r.shape, 1).astype(F32)
    oh0 = jnp.where(lane == r[:, 0:1], 1.0, 0.0)
    oh1 = jnp.where(lane == r[:, 1:2], 1.0, 0.0)
    both = oh0 + oh1
    tile_cnt = jnp.sum(both, axis=0, keepdims=True)

    @pl.when((ph == 0) & (i == 0))
    def _():
        cnt_scr[...] = jnp.zeros_like(cnt_scr)

    @pl.when(ph == 0)
    def _():
        cnt_scr[...] += tile_cnt

    @pl.when((ph == 1) & (i == 0))
    def _():
        ktiles = jnp.floor((cnt_scr[...] + (MOE_TILE - 1)) * (1.0 / MOE_TILE))
        rr = lax.broadcasted_iota(jnp.int32, (LANES, LANES), 0)
        cc = lax.broadcasted_iota(jnp.int32, (LANES, LANES), 1)
        upper = jnp.where(rr < cc, 1.0, 0.0).astype(BF16)
        first = jnp.dot(jnp.broadcast_to(ktiles, (8, LANES)).astype(BF16), upper,
                        preferred_element_type=F32)[0:1, :]
        base_scr[...] = first * MOE_TILE
        ti = lax.broadcasted_iota(jnp.int32, (ntm, LANES), 0).astype(F32)
        el = lax.broadcasted_iota(jnp.int32, (ntm, LANES), 1).astype(F32)
        owned = jnp.where(ti >= first, jnp.where(ti < first + ktiles, 1.0, 0.0), 0.0)
        texp = jnp.sum(owned * el, axis=1, keepdims=True)
        nact = jnp.sum(ktiles, axis=1, keepdims=True)
        tinfo_ref[...] = jnp.where(el == 0.0, texp, jnp.where(el == 1.0, nact, 0.0))

    @pl.when(ph == 1)
    def _():
        rr = lax.broadcasted_iota(jnp.int32, (tm, tm), 0)
        cc = lax.broadcasted_iota(jnp.int32, (tm, tm), 1)
        lower = jnp.where(cc < rr, 1.0, 0.0).astype(BF16)
        before = jnp.dot(lower, both.astype(BF16), preferred_element_type=F32)
        val = before + base_scr[...]
        p0 = jnp.sum(oh0 * val, axis=1, keepdims=True)
        p1 = jnp.sum(oh1 * val, axis=1, keepdims=True)
        pos_ref[...] = jnp.where(lane == 0.0, p0, jnp.where(lane == 1.0, p1, 0.0))
        base_scr[...] += tile_cnt


def _plan(rinfo, tm=512):
    s = rinfo.shape[0]
    ntm = _moe_num_tiles(s)
    return pl.pallas_call(
        functools.partial(_plan_kernel, tm=tm, ntm=ntm),
        out_shape=(jax.ShapeDtypeStruct((s, LANES), F32), jax.ShapeDtypeStruct((ntm, LANES), F32)),
        grid=(2, s // tm),
        in_specs=[pl.BlockSpec((tm, LANES), lambda p, i: (i, 0))],
        out_specs=(pl.BlockSpec((tm, LANES), lambda p, i: (i * p, 0)),
                   pl.BlockSpec((ntm, LANES), lambda p, i: (0, 0))),
        scratch_shapes=[pltpu.VMEM((1, LANES), F32), pltpu.VMEM((1, LANES), F32)],
        compiler_params=pltpu.CompilerParams(dimension_semantics=("arbitrary", "arbitrary"),
                                             vmem_limit_bytes=VMEM_LIMIT),
        name="plan",
    )(rinfo)


def _dispatch_kernel(pos_ref, texp_ref, nact_ref, h2_ref, xs_ref, zero_scr, sem, zsem, *, tm):
    i = pl.program_id(0)

    @pl.when(i == 0)
    def _():
        zero_scr[...] = jnp.zeros_like(zero_scr)
        nact = nact_ref[0]
        ntm = xs_ref.shape[0] // MOE_TILE

        def is_last(t):
            nxt = texp_ref[jnp.minimum(t + 1, nact - 1)]
            return (t >= nact - 1) | (nxt != texp_ref[jnp.minimum(t, nact - 1)])

        def zstart(t, carry):
            @pl.when(is_last(t))
            def _():
                pltpu.make_async_copy(zero_scr, xs_ref.at[pl.ds(pl.multiple_of(t * MOE_TILE, MOE_TILE), MOE_TILE), :],
                                      zsem).start()
            return carry

        def zwait(t, carry):
            @pl.when(is_last(t))
            def _():
                pltpu.make_async_copy(zero_scr, xs_ref.at[pl.ds(0, MOE_TILE), :], zsem).wait()
            return carry

        lax.fori_loop(0, ntm, zstart, 0)
        lax.fori_loop(0, ntm, zwait, 0)

    def start(r, carry):
        tok = i * tm + r
        for slot in range(2):
            p = pos_ref[2 * tok + slot]
            pltpu.make_async_copy(h2_ref.at[pl.ds(r, 1), :], xs_ref.at[pl.ds(p, 1), :], sem).start()
        return carry

    lax.fori_loop(0, tm, start, 0, unroll=8)
    for slot in range(2):
        pltpu.make_async_copy(h2_ref, xs_ref.at[pl.ds(0, tm), :], sem).wait()


def _dispatch(pos_flat, texp, nact, h2, tm=1024):
    s, d = h2.shape
    ntm = _moe_num_tiles(s)
    return pl.pallas_call(
        functools.partial(_dispatch_kernel, tm=tm),
        out_shape=jax.ShapeDtypeStruct((ntm * MOE_TILE, d), F32),
        grid_spec=pltpu.PrefetchScalarGridSpec(
            num_scalar_prefetch=3,
            grid=(s // tm,),
            in_specs=[pl.BlockSpec((tm, d), lambda i, *_: (i, 0))],
            out_specs=pl.BlockSpec(memory_space=pl.ANY),
            scratch_shapes=[pltpu.VMEM((MOE_TILE, d), F32), pltpu.SemaphoreType.DMA, pltpu.SemaphoreType.DMA]),
        compiler_params=pltpu.CompilerParams(dimension_semantics=("arbitrary",),
                                             vmem_limit_bytes=VMEM_LIMIT),
        name="dispatch",
    )(pos_flat, texp, nact, h2)


def _experts_kernel(texp_ref, nact_ref, xs_ref, wg_hbm, wu_hbm, wd_hbm, ys_ref,
                    wg_f32, wu_f32, wd_f32, ord_scr, sem):
    i = pl.program_id(0)
    nact = nact_ref[0]
    j = jnp.minimum(i, nact - 1)
    e = texp_ref[j]
    fresh = (i == 0) | (e != texp_ref[jnp.maximum(j - 1, 0)])

    def weight_copies(expert, slot):
        return (pltpu.make_async_copy(wg_hbm.at[expert], wg_f32.at[slot], sem.at[slot]),
                pltpu.make_async_copy(wu_hbm.at[expert], wu_f32.at[slot], sem.at[slot]),
                pltpu.make_async_copy(wd_hbm.at[expert], wd_f32.at[slot], sem.at[slot]))

    @pl.when(i == 0)
    def _():
        ord_scr[0] = 0
        for c in weight_copies(e, 0):
            c.start()

    @pl.when((i < nact) & fresh)
    def _():
        slot = ord_scr[0] % 2
        for c in weight_copies(e, slot):
            c.wait()
        nxt = lax.while_loop(lambda t: (t < nact) & (texp_ref[jnp.minimum(t, nact - 1)] == e),
                             lambda t: t + 1, i + 1)

        @pl.when(nxt < nact)
        def _():
            for c in weight_copies(texp_ref[nxt], 1 - slot):
                c.start(priority=1)
        ord_scr[0] = ord_scr[0] + 1

    @pl.when(i < nact)
    def _():
        slot = (ord_scr[0] + 1) % 2
        x = xs_ref[...].astype(BF16)
        hg = jnp.dot(x, wg_f32[slot].astype(BF16), preferred_element_type=F32)
        hu = jnp.dot(x, wu_f32[slot].astype(BF16), preferred_element_type=F32)
        a = (hg * _sigmoid(hg)) * hu
        ys_ref[...] = jnp.dot(a.astype(BF16), wd_f32[slot].astype(BF16), preferred_element_type=F32)

    @pl.when(i >= nact)
    def _():
        ys_ref[...] = jnp.zeros_like(ys_ref)


def _experts(texp, nact, xs, w_gate, w_up, w_down):
    p, d = xs.shape
    ntm = p // MOE_TILE
    _, _, f = w_gate.shape

    def tile(i, texp_ref, nact_ref):
        return (jnp.minimum(i, nact_ref[0] - 1), 0)

    hbm = pl.BlockSpec(memory_space=pl.ANY)
    return pl.pallas_call(
        _experts_kernel,
        out_shape=jax.ShapeDtypeStruct((p, d), F32),
        grid_spec=pltpu.PrefetchScalarGridSpec(
            num_scalar_prefetch=2,
            grid=(ntm,),
            in_specs=[pl.BlockSpec((MOE_TILE, d), tile), hbm, hbm, hbm],
            out_specs=pl.BlockSpec((MOE_TILE, d), lambda i, *_: (i, 0)),
            scratch_shapes=[pltpu.VMEM((2, d, f), F32), pltpu.VMEM((2, d, f), F32), pltpu.VMEM((2, f, d), F32),
                            pltpu.SMEM((1,), jnp.int32), pltpu.SemaphoreType.DMA((2,))]),
        compiler_params=pltpu.CompilerParams(dimension_semantics=("arbitrary",),
                                             vmem_limit_bytes=VMEM_LIMIT),
        name="experts",
    )(texp, nact, xs, w_gate, w_up, w_down)


def _combine_kernel(pos_ref, ys_ref, rinfo_ref, x1_ref, gpost_ref, gt_ref, o_ref, ybuf, sem, *, tm):
    i = pl.program_id(0)

    def gather(tile, b):
        def start(r, carry):
            tok = tile * tm + r
            for slot in range(2):
                p = pos_ref[2 * tok + slot]
                pltpu.make_async_copy(ys_ref.at[pl.ds(p, 1), :], ybuf.at[b, slot, pl.ds(r, 1), :],
                                      sem.at[b]).start()
            return carry
        lax.fori_loop(0, tm, start, 0, unroll=8)

    @pl.when(i == 0)
    def _():
        gather(0, 0)

    @pl.when(i + 1 < pl.num_programs(0))
    def _():
        gather(i + 1, (i + 1) % 2)

    b = i % 2
    for slot in range(2):
        pltpu.make_async_copy(ys_ref.at[pl.ds(0, tm), :], ybuf.at[b, slot], sem.at[b]).wait()
    r = rinfo_ref[...]
    y = r[:, 2:3] * ybuf[b, 0] + r[:, 3:4] * ybuf[b, 1]
    o_ref[...] = x1_ref[...] + gt_ref[...] * (_rms(y, NORM_EPS) * gpost_ref[...])


def _combine(pos_flat, ys, rinfo, x1, gpost, gt, tm=512):
    s, d = x1.shape
    row = pl.BlockSpec((1, d), lambda i, *_: (0, 0))
    return pl.pallas_call(
        functools.partial(_combine_kernel, tm=tm),
        out_shape=jax.ShapeDtypeStruct((s, d), F32),
        grid_spec=pltpu.PrefetchScalarGridSpec(
            num_scalar_prefetch=1,
            grid=(s // tm,),
            in_specs=[pl.BlockSpec(memory_space=pl.ANY),
                      pl.BlockSpec((tm, LANES), lambda i, *_: (i, 0)),
                      pl.BlockSpec((tm, d), lambda i, *_: (i, 0)),
                      row, row],
            out_specs=pl.BlockSpec((tm, d), lambda i, *_: (i, 0)),
            scratch_shapes=[pltpu.VMEM((2, 2, tm, d), F32), pltpu.SemaphoreType.DMA((2,))]),
        compiler_params=pltpu.CompilerParams(dimension_semantics=("arbitrary",),
                                             vmem_limit_bytes=VMEM_LIMIT),
        name="combine",
    )(pos_flat, ys, rinfo, x1, gpost, gt)


def _moe(h2, rinfo, w_gate, w_up, w_down, x1, gpost, gt):
    pos, tinfo = _plan(rinfo)
    pos_flat = pos[:, 0:2].astype(jnp.int32).reshape(-1)
    texp = tinfo[:, 0].astype(jnp.int32)
    nact = tinfo[0:1, 1].astype(jnp.int32)
    xs = _dispatch(pos_flat, texp, nact, h2)
    ys = _experts(texp, nact, xs, w_gate, w_up, w_down)
    return _combine(pos_flat, ys, rinfo, x1, gpost, gt)


def kernel(x, c, w_ada, b_ada, g_pre1, w_in, lam_q1, lam_k1, lam_q2, lam_k2, g_subln, w_conv, w_out,
           g_post1, g_pre2, w_router_g, b_router_g, w_router_e, b_router_e, w_gate, w_up, w_down, g_post2):
    b, s, d = x.shape
    assert b == 1
    depth = w_ada.shape[0]
    slopes = jnp.asarray([2.0 ** (-8.0 * (i + 1) / N_HEADS) for i in range(N_HEADS)], F32)
    x2 = x.reshape(s, d)
    for l in range(depth):
        lam_init = 0.8 - 0.6 * math.exp(-0.3 * l)
        mod = _ada(c.reshape(d, 1), w_ada[l], b_ada[l].reshape(1, -1))
        sh1, sc1, gt1, sh2, sc2, gt2 = [mod[:, k * d:(k + 1) * d] for k in range(6)]
        proj = _inproj(x2, g_pre1[l].reshape(1, d), sc1, sh1, w_in[l])
        y_attn = _attn(slopes, lam_q1[l].reshape(1, -1), lam_k1[l].reshape(1, -1),
                       lam_q2[l].reshape(1, -1), lam_k2[l].reshape(1, -1), proj,
                       g_subln[l].reshape(1, -1), lam_init)
        wr = jnp.concatenate([w_router_g[l], w_router_e[l].reshape(d, N_EXPERTS),
                              jnp.zeros((d, LANES - N_GROUPS - N_EXPERTS), F32)], axis=1).astype(BF16)
        br = jnp.concatenate([b_router_g[l], b_router_e[l].reshape(N_EXPERTS),
                              jnp.zeros((LANES - N_GROUPS - N_EXPERTS,), F32)]).reshape(1, LANES)
        x1, h2, rinfo = _outproj(y_attn, proj, x2, w_conv[l], w_out[l].astype(BF16),
                                 g_post1[l].reshape(1, d), gt1, g_pre2[l].reshape(1, d), sc2, sh2, wr, br)
        x2 = _moe(h2, rinfo, w_gate[l], w_up[l], w_down[l], x1, g_post2[l].reshape(1, d), gt2)
    return x2.reshape(b, s, d)
```

```python
import functools
import math

import jax
import jax.numpy as jnp
from jax import lax
from jax.experimental import pallas as pl
from jax.experimental.pallas import tpu as pltpu

F32 = jnp.float32
BF16 = jnp.bfloat16

LANES = 128
CHUNK = 64
N_HEADS = 8
QK_DIM = 64
V_DIM = 128
CONV_WIDTH = 3
N_GROUPS = 4
EXPERTS_PER_GROUP = 8
N_EXPERTS = N_GROUPS * EXPERTS_PER_GROUP
NORM_EPS = 1e-6
SUBLN_EPS = 1e-5
NEG_BIG = -1e30
LOG2E = 1.4426950408889634
VMEM_LIMIT = 56 * 1024 * 1024


def _rms(x, eps):
    return x * lax.rsqrt(jnp.mean(x * x, axis=-1, keepdims=True) + eps)


def _sigmoid(x):
    return 1.0 / (1.0 + jnp.exp(-x))


def _ada_kernel(c_ref, w_ref, b_ref, o_ref):
    c = c_ref[...]
    s = c * _sigmoid(c)
    o_ref[...] = jnp.sum(w_ref[...] * s, axis=0, keepdims=True) + b_ref[...]


def _ada(c_col, w_ada, b_ada, tn=2048):
    d, n = w_ada.shape
    return pl.pallas_call(
        _ada_kernel,
        out_shape=jax.ShapeDtypeStruct((1, n), F32),
        grid=(n // tn,),
        in_specs=[pl.BlockSpec((d, 1), lambda j: (0, 0)),
                  pl.BlockSpec((d, tn), lambda j: (0, j)),
                  pl.BlockSpec((1, tn), lambda j: (0, j))],
        out_specs=pl.BlockSpec((1, tn), lambda j: (0, j)),
        compiler_params=pltpu.CompilerParams(dimension_semantics=("arbitrary",),
                                             vmem_limit_bytes=VMEM_LIMIT),
        name="ada",
    )(c_col, w_ada, b_ada)


def _inproj_kernel(x_ref, g_ref, sc_ref, sh_ref, w_ref, o_ref, h_scr, *, cn):
    @pl.when(pl.program_id(1) == 0)
    def _():
        rows = 16

        scale = g_ref[...] * (1.0 + sc_ref[...])

        def norm_rows(r, carry):
            sl = pl.ds(pl.multiple_of(r * rows, rows), rows)
            h_scr[sl, :] = (_rms(x_ref[sl, :], NORM_EPS) * scale + sh_ref[...]).astype(BF16)
            return carry
        lax.fori_loop(0, x_ref.shape[0] // rows, norm_rows, 0, unroll=8)

    acc = jnp.dot(h_scr[...], w_ref[...].astype(BF16), preferred_element_type=F32)
    for c in range(cn):
        o_ref[c] = acc[:, c * LANES:(c + 1) * LANES].astype(BF16)


def _inproj(x2, g, sc, sh, w, tm=1024, cn=12):
    s, d = x2.shape
    n = w.shape[1]
    nchunks = n // LANES
    row = lambda i, j: (0, 0)
    return pl.pallas_call(
        functools.partial(_inproj_kernel, cn=cn),
        out_shape=jax.ShapeDtypeStruct((nchunks, s, LANES), BF16),
        grid=(s // tm, nchunks // cn),
        in_specs=[pl.BlockSpec((tm, d), lambda i, j: (i, 0)),
                  pl.BlockSpec((1, d), row),
                  pl.BlockSpec((1, d), row),
                  pl.BlockSpec((1, d), row),
                  pl.BlockSpec((d, cn * LANES), lambda i, j: (0, j))],
        out_specs=pl.BlockSpec((cn, tm, LANES), lambda i, j: (j, i, 0)),
        scratch_shapes=[pltpu.VMEM((tm, d), BF16)],
        compiler_params=pltpu.CompilerParams(dimension_semantics=("arbitrary", "arbitrary"),
                                             vmem_limit_bytes=VMEM_LIMIT),
        name="inproj",
    )(x2, g, sc, sh, w)


ATT_POS = 256
ATT_VROWS = 144


def _attn_kernel(slope_ref, lq1_ref, lk1_ref, lq2_ref, lk2_ref, q_ref, k_ref, v_ref, gsub_ref,
                 o_ref, vt_scr, sa_scr, sb_scr, m_scr, acc_scr, corr_scr, *, t, lam_init):
    h = pl.program_id(0)
    nkt = vt_scr.shape[0]
    slope2 = slope_ref[h] * LOG2E
    c_hi = jnp.full((LANES, t), slope2, F32).astype(BF16).astype(F32)
    c_lo = (jnp.full((LANES, t), slope2, F32) - c_hi).astype(BF16).astype(F32)

    ones_row = jnp.where(lax.broadcasted_iota(jnp.int32, (ATT_VROWS - V_DIM, t), 0) == 0, 1.0, 0.0)

    def tr(c, carry):
        vc = v_ref[0, pl.ds(pl.multiple_of(c * t, t), t), :]
        vt_scr[c] = jnp.concatenate([vc.astype(F32).T, ones_row], axis=0).astype(BF16)
        return carry
    lax.fori_loop(0, nkt, tr, 0)

    krel = lax.broadcasted_iota(jnp.int32, (t, t), 0)
    qrel = lax.broadcasted_iota(jnp.int32, (t, t), 1)
    shift = CHUNK.bit_length() - 1
    allowed = (krel >> shift) <= (qrel >> shift)
    c_mxu = (c_hi + c_lo)[0:1, 0:1]
    corr_scr[...] = jnp.where(allowed, slope2 * (qrel - jnp.abs(qrel - krel)).astype(F32)
                              - c_mxu * krel.astype(F32), NEG_BIG)

    step_dec = slope2 * t
    row = lax.broadcasted_iota(jnp.int32, (LANES, t), 0)
    qfeat = jnp.where(row == 0, c_hi, jnp.where(row == 1, ATT_POS * c_hi,
                      jnp.where(row == 2, c_lo, jnp.where(row == 3, ATT_POS * c_lo, 0.0))))
    lam = (jnp.exp(jnp.sum(lq1_ref[...] * lk1_ref[...], axis=-1, keepdims=True))
           - jnp.exp(jnp.sum(lq2_ref[...] * lk2_ref[...], axis=-1, keepdims=True)) + lam_init)

    def colmax(s):
        while s.shape[0] > 8:
            half = s.shape[0] // 2
            s = jnp.maximum(s[:half], s[half:])
        return jnp.max(s, axis=0, keepdims=True)

    def query_tile(qi, carry):
        rows = pl.ds(pl.multiple_of(qi * t, t), t)
        qt = (q_ref[0, rows, :].astype(F32) * (LOG2E * QK_DIM ** -0.5)).T
        qa = jnp.concatenate([jnp.where(row < QK_DIM, qt, 0.0), qfeat], axis=0)
        qb = jnp.concatenate([jnp.where(row >= QK_DIM, qt, 0.0), qfeat], axis=0)
        qts = jnp.concatenate([qa, qb], axis=1).astype(BF16)

        krow = lax.broadcasted_iota(jnp.int32, (t, LANES), 0)
        klane = lax.broadcasted_iota(jnp.int32, (t, LANES), 1)
        pbits = ATT_POS.bit_length() - 1
        kmod = (krow & (ATT_POS - 1)).astype(F32)
        kdiv = (krow >> pbits).astype(F32)
        kfeat = jnp.where((klane == 0) | (klane == 2), kmod,
                          jnp.where((klane == 1) | (klane == 3), kdiv, 0.0)).astype(BF16)

        m_scr[...] = jnp.full(m_scr.shape, NEG_BIG, F32)
        acc_scr[...] = jnp.zeros(acc_scr.shape, F32)

        def scores(j, s_ref):
            kj = k_ref[0, pl.ds(pl.multiple_of(j * t, t), t), :]
            s_ref[...] = jnp.dot(jnp.concatenate([kj, kfeat], axis=1), qts, preferred_element_type=F32)

        def softmax_pv(j, s_ref, corr):
            s = s_ref[...]
            if corr is not None:
                s = s + corr
            m_old = m_scr[...] - step_dec
            m_new = jnp.maximum(m_old, colmax(s))
            alpha = jnp.exp2(m_old - m_new)
            p = jnp.exp2(s - m_new).astype(BF16)
            pv = jnp.dot(vt_scr[j], p, preferred_element_type=F32)
            acc_scr[...] = alpha * acc_scr[...] + pv
            m_scr[...] = m_new

        def near(s_ref):
            corr = corr_scr[...]
            softmax_pv(qi, s_ref, jnp.concatenate([corr, corr], axis=1))

        scores(0, sa_scr)

        def pair(i, c):
            scores(2 * i + 1, sb_scr)
            softmax_pv(2 * i, sa_scr, None)
            scores(2 * i + 2, sa_scr)
            softmax_pv(2 * i + 1, sb_scr, None)
            return c

        def quad(i, c):
            return pair(2 * i + 1, pair(2 * i, c))

        def octo(i, c):
            return quad(2 * i + 1, quad(2 * i, c))

        lax.fori_loop(0, qi // 8, octo, 0)
        lax.fori_loop(2 * (qi // 8), qi // 4, quad, 0)
        lax.fori_loop(2 * (qi // 4), qi // 2, pair, 0)

        @pl.when(qi % 2 == 1)
        def _():
            scores(qi, sb_scr)
            softmax_pv(qi - 1, sa_scr, None)
            near(sb_scr)

        @pl.when(qi % 2 == 0)
        def _():
            near(sa_scr)

        o1 = acc_scr[0:V_DIM, 0:t] / acc_scr[V_DIM:V_DIM + 1, 0:t]
        o2 = acc_scr[0:V_DIM, t:2 * t] / acc_scr[V_DIM:V_DIM + 1, t:2 * t]
        ot = o1 - lam * o2
        ot = ot * lax.rsqrt(jnp.mean(ot * ot, axis=0, keepdims=True) + SUBLN_EPS)
        o_ref[0, rows, :] = (ot.T * gsub_ref[...] * (1.0 - lam_init)).astype(BF16)
        return carry

    lax.fori_loop(0, nkt, query_tile, 0)


def _attn(slopes, lq1, lk1, lq2, lk2, proj, gsub, lam_init, t=512):
    s = proj.shape[1]
    smem = pl.BlockSpec(memory_space=pltpu.SMEM)
    vec = lambda n: pl.BlockSpec((1, n), lambda h: (0, 0))
    head = lambda base: pl.BlockSpec((1, s, LANES), lambda h, base=base: (base + h, 0, 0))
    return pl.pallas_call(
        functools.partial(_attn_kernel, t=t, lam_init=lam_init),
        out_shape=jax.ShapeDtypeStruct((N_HEADS, s, V_DIM), BF16),
        grid=(N_HEADS,),
        in_specs=[smem, vec(QK_DIM), vec(QK_DIM), vec(QK_DIM), vec(QK_DIM),
                  head(0), head(N_HEADS), head(2 * N_HEADS), vec(V_DIM)],
        out_specs=pl.BlockSpec((1, s, V_DIM), lambda h: (h, 0, 0)),
        scratch_shapes=[pltpu.VMEM((s // t, ATT_VROWS, t), BF16),
                        pltpu.VMEM((t, 2 * t), F32), pltpu.VMEM((t, 2 * t), F32),
                        pltpu.VMEM((1, 2 * t), F32),
                        pltpu.VMEM((ATT_VROWS, 2 * t), F32),
                        pltpu.VMEM((t, t), F32)],
        compiler_params=pltpu.CompilerParams(dimension_semantics=("arbitrary",),
                                             vmem_limit_bytes=VMEM_LIMIT),
        name="attn",
    )(slopes, lq1, lk1, lq2, lk2, proj, proj, proj, gsub)


def _route(logits):
    lane = lax.broadcasted_iota(jnp.int32, logits.shape, 1).astype(F32)
    far = float(4 * LANES)
    gl = jnp.where(lane < N_GROUPS, logits, NEG_BIG)
    gmax = jnp.max(gl, axis=-1, keepdims=True)
    g_w = 1.0 / jnp.sum(jnp.exp(gl - gmax), axis=-1, keepdims=True)
    g_idx = jnp.min(jnp.where(gl == gmax, lane, far), axis=-1, keepdims=True)
    lo = N_GROUPS + EXPERTS_PER_GROUP * g_idx
    el = jnp.where((lane >= lo) & (lane < lo + EXPERTS_PER_GROUP), logits, NEG_BIG)
    e1 = jnp.max(el, axis=-1, keepdims=True)
    i1 = jnp.min(jnp.where(el == e1, lane, far), axis=-1, keepdims=True)
    el2 = jnp.where(lane == i1, NEG_BIG, el)
    e2 = jnp.max(el2, axis=-1, keepdims=True)
    i2 = jnp.min(jnp.where(el2 == e2, lane, far), axis=-1, keepdims=True)
    r = jnp.exp(e2 - e1)
    w1 = 1.0 / (1.0 + r)
    w2 = r * w1
    out = jnp.where(lane == 0.0, i1 - N_GROUPS, 0.0)
    out = jnp.where(lane == 1.0, i2 - N_GROUPS, out)
    out = jnp.where(lane == 2.0, g_w * w1, out)
    out = jnp.where(lane == 3.0, g_w * w2, out)
    return out


def _outproj_kernel(ya_ref, b_ref, c_ref, u_ref, x_ref, wconv_ref, wout_ref, gpost_ref, gt_ref,
                    gpre_ref, sc_ref, sh_ref, wr_ref, br_ref,
                    x1_ref, h2_ref, rinfo_ref, zbuf, ybuf, *, tm, nc):
    i = pl.program_id(0)
    halo = 8

    @pl.when(i == 0)
    def _():
        zbuf[0:halo, :] = jnp.zeros((halo, zbuf.shape[1]), F32)

    @pl.when(i > 0)
    def _():
        zbuf[0:halo, :] = zbuf[tm:tm + halo, :]

    for c in range(nc):
        cols = slice(c * LANES, (c + 1) * LANES)
        zbuf[halo:halo + tm, cols] = c_ref[c].astype(F32) * u_ref[c].astype(F32)
    d_attn = ya_ref.shape[0] * LANES
    nh = 2
    hm = tm // nh
    for part in range(nh):
        r0 = part * hm
        rows = slice(r0, r0 + hm)
        for c in range(ya_ref.shape[0]):
            ybuf[rows, c * LANES:(c + 1) * LANES] = ya_ref[c, rows, :]
        for c in range(nc):
            cols = slice(c * LANES, (c + 1) * LANES)
            conv = (wconv_ref[0:1, cols] * zbuf[r0 + halo - 2:r0 + halo - 2 + hm, cols]
                    + wconv_ref[1:2, cols] * zbuf[r0 + halo - 1:r0 + halo - 1 + hm, cols]
                    + wconv_ref[2:3, cols] * zbuf[r0 + halo:r0 + halo + hm, cols])
            ybuf[rows, d_attn + c * LANES:d_attn + (c + 1) * LANES] = (
                b_ref[c, rows, :].astype(F32) * conv).astype(BF16)

        y = jnp.dot(ybuf[rows, :], wout_ref[...], preferred_element_type=F32)
        x1 = x_ref[rows, :] + gt_ref[...] * (_rms(y, NORM_EPS) * gpost_ref[...])
        x1_ref[rows, :] = x1
        h2 = (_rms(x1, NORM_EPS) * gpre_ref[...]) * (1.0 + sc_ref[...]) + sh_ref[...]
        h2_ref[rows, :] = h2
        logits = jnp.dot(h2.astype(BF16), wr_ref[...], preferred_element_type=F32) + br_ref[...]
        rinfo_ref[rows, :] = _route(logits)


def _outproj(y_attn, proj, x2, w_conv, wout_bf, gpost, gt, gpre, sc, sh, wr_bf, br, tm=512):
    s, d = x2.shape
    nh = y_attn.shape[0]
    d_attn = nh * LANES
    nc = (d - d_attn) // LANES
    base = 3 * d_attn // LANES
    row = lambda n: pl.BlockSpec((1, n), lambda i: (0, 0))
    blk = lambda g: pl.BlockSpec((nc, tm, LANES), lambda i, g=g: (base // nc + g, i, 0))
    return pl.pallas_call(
        functools.partial(_outproj_kernel, tm=tm, nc=nc),
        out_shape=(jax.ShapeDtypeStruct((s, d), F32),
                   jax.ShapeDtypeStruct((s, d), F32),
                   jax.ShapeDtypeStruct((s, LANES), F32)),
        grid=(s // tm,),
        in_specs=[pl.BlockSpec((nh, tm, LANES), lambda i: (0, i, 0)),
                  blk(0), blk(1), blk(2),
                  pl.BlockSpec((tm, d), lambda i: (i, 0)),
                  pl.BlockSpec((CONV_WIDTH, d - d_attn), lambda i: (0, 0)),
                  pl.BlockSpec((d, d), lambda i: (0, 0)),
                  row(d), row(d), row(d), row(d), row(d),
                  pl.BlockSpec((d, LANES), lambda i: (0, 0)),
                  row(LANES)],
        out_specs=(pl.BlockSpec((tm, d), lambda i: (i, 0)),
                   pl.BlockSpec((tm, d), lambda i: (i, 0)),
                   pl.BlockSpec((tm, LANES), lambda i: (i, 0))),
        scratch_shapes=[pltpu.VMEM((tm + 8, d - d_attn), F32), pltpu.VMEM((tm, d), BF16)],
        compiler_params=pltpu.CompilerParams(dimension_semantics=("arbitrary",),
                                             vmem_limit_bytes=VMEM_LIMIT),
        name="outproj",
    )(y_attn, proj, proj, proj, x2, w_conv, wout_bf, gpost, gt, gpre, sc, sh, wr_bf, br)


MOE_TILE = 256
EXPERT_TILES_PER_STEP = 2


def _moe_num_tiles(s):
    return (2 * s) // MOE_TILE + N_EXPERTS


def _plan_kernel(rinfo_ref, pos_ref, tinfo_ref, cnt_scr, base_scr, *, tm, ntm):
    ph = pl.program_id(0)
    i = pl.program_id(1)
    r = rinfo_ref[...]
    lane = lax.broadcasted_iota(jnp.int32, r.shape, 1).astype(F32)
    oh0 = jnp.where(lane == r[:, 0:1], 1.0, 0.0)
    oh1 = jnp.where(lane == r[:, 1:2], 1.0, 0.0)
    both = oh0 + oh1
    tile_cnt = jnp.sum(both, axis=0, keepdims=True)

    @pl.when((ph == 0) & (i == 0))
    def _():
        cnt_scr[...] = jnp.zeros_like(cnt_scr)

    @pl.when(ph == 0)
    def _():
        cnt_scr[...] += tile_cnt

    @pl.when((ph == 1) & (i == 0))
    def _():
        ktiles = jnp.floor((cnt_scr[...] + (MOE_TILE - 1)) * (1.0 / MOE_TILE))
        rr = lax.broadcasted_iota(jnp.int32, (LANES, LANES), 0)
        cc = lax.broadcasted_iota(jnp.int32, (LANES, LANES), 1)
        upper = jnp.where(rr < cc, 1.0, 0.0).astype(BF16)
        first = jnp.dot(jnp.broadcast_to(ktiles, (8, LANES)).astype(BF16), upper,
                        preferred_element_type=F32)[0:1, :]
        base_scr[...] = first * MOE_TILE
        ti = lax.broadcasted_iota(jnp.int32, (ntm, LANES), 0).astype(F32)
        el = lax.broadcasted_iota(jnp.int32, (ntm, LANES), 1).astype(F32)
        owned = jnp.where(ti >= first, jnp.where(ti < first + ktiles, 1.0, 0.0), 0.0)
        texp = jnp.sum(owned * el, axis=1, keepdims=True)
        nact = jnp.sum(ktiles, axis=1, keepdims=True)
        tinfo_ref[...] = jnp.where(el == 0.0, texp, jnp.where(el == 1.0, nact, 0.0))

    @pl.when(ph == 1)
    def _():
        rr = lax.broadcasted_iota(jnp.int32, (tm, tm), 0)
        cc = lax.broadcasted_iota(jnp.int32, (tm, tm), 1)
        lower = jnp.where(cc < rr, 1.0, 0.0).astype(BF16)
        before = jnp.dot(lower, both.astype(BF16), preferred_element_type=F32)
        val = before + base_scr[...]
        p0 = jnp.sum(oh0 * val, axis=1, keepdims=True)
        p1 = jnp.sum(oh1 * val, axis=1, keepdims=True)
        pos_ref[...] = jnp.where(lane == 0.0, p0, jnp.where(lane == 1.0, p1, 0.0))
        base_scr[...] += tile_cnt


def _plan(rinfo, tm=512):
    s = rinfo.shape[0]
    ntm = _moe_num_tiles(s)
    return pl.pallas_call(
        functools.partial(_plan_kernel, tm=tm, ntm=ntm),
        out_shape=(jax.ShapeDtypeStruct((s, LANES), F32), jax.ShapeDtypeStruct((ntm, LANES), F32)),
        grid=(2, s // tm),
        in_specs=[pl.BlockSpec((tm, LANES), lambda p, i: (i, 0))],
        out_specs=(pl.BlockSpec((tm, LANES), lambda p, i: (i * p, 0)),
                   pl.BlockSpec((ntm, LANES), lambda p, i: (0, 0))),
        scratch_shapes=[pltpu.VMEM((1, LANES), F32), pltpu.VMEM((1, LANES), F32)],
        compiler_params=pltpu.CompilerParams(dimension_semantics=("arbitrary", "arbitrary"),
                                             vmem_limit_bytes=VMEM_LIMIT),
        name="plan",
    )(rinfo)


def _dispatch_kernel(pos_ref, texp_ref, nact_ref, h2_ref, xs_ref, zero_scr, sem, zsem, *, tm):
    i = pl.program_id(0)

    @pl.when(i == 0)
    def _():
        zero_scr[...] = jnp.zeros_like(zero_scr)
        nact = nact_ref[0]
        ntm = xs_ref.shape[0] // MOE_TILE

        def is_last(t):
            nxt = texp_ref[jnp.minimum(t + 1, nact - 1)]
            return (t >= nact - 1) | (nxt != texp_ref[jnp.minimum(t, nact - 1)])

        def zstart(t, carry):
            @pl.when(is_last(t))
            def _():
                pltpu.make_async_copy(zero_scr, xs_ref.at[pl.ds(pl.multiple_of(t * MOE_TILE, MOE_TILE), MOE_TILE), :],
                                      zsem).start()
            return carry

        def zwait(t, carry):
            @pl.when(is_last(t))
            def _():
                pltpu.make_async_copy(zero_scr, xs_ref.at[pl.ds(0, MOE_TILE), :], zsem).wait()
            return carry

        lax.fori_loop(0, ntm, zstart, 0)
        lax.fori_loop(0, ntm, zwait, 0)

    def start(r, carry):
        tok = i * tm + r
        for slot in range(2):
            p = pos_ref[2 * tok + slot]
            pltpu.make_async_copy(h2_ref.at[pl.ds(r, 1), :], xs_ref.at[pl.ds(p, 1), :], sem).start()
        return carry

    lax.fori_loop(0, tm, start, 0, unroll=8)
    for slot in range(2):
        pltpu.make_async_copy(h2_ref, xs_ref.at[pl.ds(0, tm), :], sem).wait()


def _dispatch(pos_flat, texp, nact, h2, tm=1024):
    s, d = h2.shape
    ntm = _moe_num_tiles(s)
    return pl.pallas_call(
        functools.partial(_dispatch_kernel, tm=tm),
        out_shape=jax.ShapeDtypeStruct((ntm * MOE_TILE, d), F32),
        grid_spec=pltpu.PrefetchScalarGridSpec(
            num_scalar_prefetch=3,
            grid=(s // tm,),
            in_specs=[pl.BlockSpec((tm, d), lambda i, *_: (i, 0))],
            out_specs=pl.BlockSpec(memory_space=pl.ANY),
            scratch_shapes=[pltpu.VMEM((MOE_TILE, d), F32), pltpu.SemaphoreType.DMA, pltpu.SemaphoreType.DMA]),
        compiler_params=pltpu.CompilerParams(dimension_semantics=("arbitrary",),
                                             vmem_limit_bytes=VMEM_LIMIT),
        name="dispatch",
    )(pos_flat, texp, nact, h2)


def _experts_kernel(texp_ref, nact_ref, xs_ref, wg_hbm, wu_hbm, wd_hbm, ys_ref,
                    wg_f32, wu_f32, wd_f32, ord_scr, sem):
    nact = nact_ref[0]

    def weight_copies(expert, slot):
        return (pltpu.make_async_copy(wg_hbm.at[expert], wg_f32.at[slot], sem.at[slot]),
                pltpu.make_async_copy(wu_hbm.at[expert], wu_f32.at[slot], sem.at[slot]),
                pltpu.make_async_copy(wd_hbm.at[expert], wd_f32.at[slot], sem.at[slot]))

    def one_tile(i, rows):
        j = jnp.minimum(i, nact - 1)
        e = texp_ref[j]
        fresh = (i == 0) | (e != texp_ref[jnp.maximum(j - 1, 0)])

        @pl.when(i == 0)
        def _():
            ord_scr[0] = 0
            for c in weight_copies(e, 0):
                c.start()

        @pl.when((i < nact) & fresh)
        def _():
            slot = ord_scr[0] % 2
            for c in weight_copies(e, slot):
                c.wait()
            nxt = lax.while_loop(lambda t: (t < nact) & (texp_ref[jnp.minimum(t, nact - 1)] == e),
                                 lambda t: t + 1, i + 1)

            @pl.when(nxt < nact)
            def _():
                for c in weight_copies(texp_ref[nxt], 1 - slot):
                    c.start(priority=1)
            ord_scr[0] = ord_scr[0] + 1

        @pl.when(i < nact)
        def _():
            slot = (ord_scr[0] + 1) % 2
            x = xs_ref[rows, :].astype(BF16)
            hg = jnp.dot(x, wg_f32[slot].astype(BF16), preferred_element_type=F32)
            hu = jnp.dot(x, wu_f32[slot].astype(BF16), preferred_element_type=F32)
            a = (hg * _sigmoid(hg)) * hu
            ys_ref[rows, :] = jnp.dot(a.astype(BF16), wd_f32[slot].astype(BF16), preferred_element_type=F32)

        @pl.when(i >= nact)
        def _():
            ys_ref[rows, :] = jnp.zeros((MOE_TILE, ys_ref.shape[1]), F32)

    for sub in range(EXPERT_TILES_PER_STEP):
        one_tile(pl.program_id(0) * EXPERT_TILES_PER_STEP + sub, slice(sub * MOE_TILE, (sub + 1) * MOE_TILE))


def _experts(texp, nact, xs, w_gate, w_up, w_down):
    p, d = xs.shape
    ntm = p // MOE_TILE
    _, _, f = w_gate.shape

    tps = EXPERT_TILES_PER_STEP
    assert ntm % tps == 0

    def tile(i, texp_ref, nact_ref):
        return (jnp.minimum(i, (nact_ref[0] - 1) // tps), 0)

    hbm = pl.BlockSpec(memory_space=pl.ANY)
    return pl.pallas_call(
        _experts_kernel,
        out_shape=jax.ShapeDtypeStruct((p, d), F32),
        grid_spec=pltpu.PrefetchScalarGridSpec(
            num_scalar_prefetch=2,
            grid=(ntm // tps,),
            in_specs=[pl.BlockSpec((tps * MOE_TILE, d), tile), hbm, hbm, hbm],
            out_specs=pl.BlockSpec((tps * MOE_TILE, d), lambda i, *_: (i, 0)),
            scratch_shapes=[pltpu.VMEM((2, d, f), F32), pltpu.VMEM((2, d, f), F32), pltpu.VMEM((2, f, d), F32),
                            pltpu.SMEM((1,), jnp.int32), pltpu.SemaphoreType.DMA((2,))]),
        compiler_params=pltpu.CompilerParams(dimension_semantics=("arbitrary",),
                                             vmem_limit_bytes=VMEM_LIMIT),
        name="experts",
    )(texp, nact, xs, w_gate, w_up, w_down)


def _combine_kernel(pos_ref, ys_ref, rinfo_ref, x1_ref, gpost_ref, gt_ref, o_ref, ybuf, sem, *, tm):
    i = pl.program_id(0)

    def gather(tile, b):
        def start(r, carry):
            tok = tile * tm + r
            for slot in range(2):
                p = pos_ref[2 * tok + slot]
                pltpu.make_async_copy(ys_ref.at[pl.ds(p, 1), :], ybuf.at[b, slot, pl.ds(r, 1), :],
                                      sem.at[b]).start()
            return carry
        lax.fori_loop(0, tm, start, 0, unroll=8)

    @pl.when(i == 0)
    def _():
        gather(0, 0)

    @pl.when(i + 1 < pl.num_programs(0))
    def _():
        gather(i + 1, (i + 1) % 2)

    b = i % 2
    for slot in range(2):
        pltpu.make_async_copy(ys_ref.at[pl.ds(0, tm), :], ybuf.at[b, slot], sem.at[b]).wait()
    r = rinfo_ref[...]
    y = r[:, 2:3] * ybuf[b, 0] + r[:, 3:4] * ybuf[b, 1]
    o_ref[...] = x1_ref[...] + gt_ref[...] * (_rms(y, NORM_EPS) * gpost_ref[...])


def _combine(pos_flat, ys, rinfo, x1, gpost, gt, tm=512):
    s, d = x1.shape
    row = pl.BlockSpec((1, d), lambda i, *_: (0, 0))
    return pl.pallas_call(
        functools.partial(_combine_kernel, tm=tm),
        out_shape=jax.ShapeDtypeStruct((s, d), F32),
        grid_spec=pltpu.PrefetchScalarGridSpec(
            num_scalar_prefetch=1,
            grid=(s // tm,),
            in_specs=[pl.BlockSpec(memory_space=pl.ANY),
                      pl.BlockSpec((tm, LANES), lambda i, *_: (i, 0)),
                      pl.BlockSpec((tm, d), lambda i, *_: (i, 0)),
                      row, row],
            out_specs=pl.BlockSpec((tm, d), lambda i, *_: (i, 0)),
            scratch_shapes=[pltpu.VMEM((2, 2, tm, d), F32), pltpu.SemaphoreType.DMA((2,))]),
        compiler_params=pltpu.CompilerParams(dimension_semantics=("arbitrary",),
                                             vmem_limit_bytes=VMEM_LIMIT),
        name="combine",
    )(pos_flat, ys, rinfo, x1, gpost, gt)


def _moe(h2, rinfo, w_gate, w_up, w_down, x1, gpost, gt):
    pos, tinfo = _plan(rinfo)
    pos_flat = pos[:, 0:2].astype(jnp.int32).reshape(-1)
    texp = tinfo[:, 0].astype(jnp.int32)
    nact = tinfo[0:1, 1].astype(jnp.int32)
    xs = _dispatch(pos_flat, texp, nact, h2)
    ys = _experts(texp, nact, xs, w_gate, w_up, w_down)
    return _combine(pos_flat, ys, rinfo, x1, gpost, gt)


def kernel(x, c, w_ada, b_ada, g_pre1, w_in, lam_q1, lam_k1, lam_q2, lam_k2, g_subln, w_conv, w_out,
           g_post1, g_pre2, w_router_g, b_router_g, w_router_e, b_router_e, w_gate, w_up, w_down, g_post2):
    b, s, d = x.shape
    assert b == 1
    depth = w_ada.shape[0]
    slopes = jnp.asarray([2.0 ** (-8.0 * (i + 1) / N_HEADS) for i in range(N_HEADS)], F32)
    x2 = x.reshape(s, d)
    for l in range(depth):
        lam_init = 0.8 - 0.6 * math.exp(-0.3 * l)
        mod = _ada(c.reshape(d, 1), w_ada[l], b_ada[l].reshape(1, -1))
        sh1, sc1, gt1, sh2, sc2, gt2 = [mod[:, k * d:(k + 1) * d] for k in range(6)]
        proj = _inproj(x2, g_pre1[l].reshape(1, d), sc1, sh1, w_in[l])
        y_attn = _attn(slopes, lam_q1[l].reshape(1, -1), lam_k1[l].reshape(1, -1),
                       lam_q2[l].reshape(1, -1), lam_k2[l].reshape(1, -1), proj,
                       g_subln[l].reshape(1, -1), lam_init)
        wr = jnp.concatenate([w_router_g[l], w_router_e[l].reshape(d, N_EXPERTS),
                              jnp.zeros((d, LANES - N_GROUPS - N_EXPERTS), F32)], axis=1).astype(BF16)
        br = jnp.concatenate([b_router_g[l], b_router_e[l].reshape(N_EXPERTS),
                              jnp.zeros((LANES - N_GROUPS - N_EXPERTS,), F32)]).reshape(1, LANES)
        x1, h2, rinfo = _outproj(y_attn, proj, x2, w_conv[l], w_out[l].astype(BF16),
                                 g_post1[l].reshape(1, d), gt1, g_pre2[l].reshape(1, d), sc2, sh2, wr, br)
        x2 = _moe(h2, rinfo, w_gate[l], w_up[l], w_down[l], x1, g_post2[l].reshape(1, d), gt2)
    return x2.reshape(b, s, d)
```

```python
import functools
import math

import jax
import jax.numpy as jnp
from jax import lax
from jax.experimental import pallas as pl
from jax.experimental.pallas import tpu as pltpu

F32 = jnp.float32
BF16 = jnp.bfloat16

LANES = 128
CHUNK = 64
N_HEADS = 8
QK_DIM = 64
V_DIM = 128
CONV_WIDTH = 3
N_GROUPS = 4
EXPERTS_PER_GROUP = 8
N_EXPERTS = N_GROUPS * EXPERTS_PER_GROUP
NORM_EPS = 1e-6
SUBLN_EPS = 1e-5
NEG_BIG = -1e30
LOG2E = 1.4426950408889634
VMEM_LIMIT = 56 * 1024 * 1024


def _rms(x, eps):
    return x * lax.rsqrt(jnp.mean(x * x, axis=-1, keepdims=True) + eps)


def _sigmoid(x):
    return 1.0 / (1.0 + jnp.exp(-x))


def _ada_kernel(c_ref, w_ref, b_ref, o_ref):
    c = c_ref[...]
    s = c * _sigmoid(c)
    o_ref[...] = jnp.sum(w_ref[...] * s, axis=0, keepdims=True) + b_ref[...]


def _ada(c_col, w_ada, b_ada, tn=1024):
    d, n = w_ada.shape
    return pl.pallas_call(
        _ada_kernel,
        out_shape=jax.ShapeDtypeStruct((1, n), F32),
        grid=(n // tn,),
        in_specs=[pl.BlockSpec((d, 1), lambda j: (0, 0)),
                  pl.BlockSpec((d, tn), lambda j: (0, j)),
                  pl.BlockSpec((1, tn), lambda j: (0, j))],
        out_specs=pl.BlockSpec((1, tn), lambda j: (0, j)),
        compiler_params=pltpu.CompilerParams(dimension_semantics=("arbitrary",),
                                             vmem_limit_bytes=VMEM_LIMIT),
        name="ada",
    )(c_col, w_ada, b_ada)


def _inproj_kernel(x_ref, g_ref, sc_ref, sh_ref, w_ref, o_ref, h_scr, *, cn):
    @pl.when(pl.program_id(1) == 0)
    def _():
        rows = 16

        scale = g_ref[...] * (1.0 + sc_ref[...])

        def norm_rows(r, carry):
            sl = pl.ds(pl.multiple_of(r * rows, rows), rows)
            h_scr[sl, :] = (_rms(x_ref[sl, :], NORM_EPS) * scale + sh_ref[...]).astype(BF16)
            return carry
        lax.fori_loop(0, x_ref.shape[0] // rows, norm_rows, 0, unroll=8)

    acc = jnp.dot(h_scr[...], w_ref[...].astype(BF16), preferred_element_type=F32)
    for c in range(cn):
        o_ref[c] = acc[:, c * LANES:(c + 1) * LANES].astype(BF16)


def _inproj(x2, g, sc, sh, w, tm=1024, cn=12):
    s, d = x2.shape
    n = w.shape[1]
    nchunks = n // LANES
    row = lambda i, j: (0, 0)
    return pl.pallas_call(
        functools.partial(_inproj_kernel, cn=cn),
        out_shape=jax.ShapeDtypeStruct((nchunks, s, LANES), BF16),
        grid=(s // tm, nchunks // cn),
        in_specs=[pl.BlockSpec((tm, d), lambda i, j: (i, 0)),
                  pl.BlockSpec((1, d), row),
                  pl.BlockSpec((1, d), row),
                  pl.BlockSpec((1, d), row),
                  pl.BlockSpec((d, cn * LANES), lambda i, j: (0, j))],
        out_specs=pl.BlockSpec((cn, tm, LANES), lambda i, j: (j, i, 0)),
        scratch_shapes=[pltpu.VMEM((tm, d), BF16)],
        compiler_params=pltpu.CompilerParams(dimension_semantics=("arbitrary", "arbitrary"),
                                             vmem_limit_bytes=VMEM_LIMIT),
        name="inproj",
    )(x2, g, sc, sh, w)


ATT_POS = 256
ATT_VROWS = 144


def _attn_kernel(slope_ref, lq1_ref, lk1_ref, lq2_ref, lk2_ref, q_ref, k_ref, v_ref, gsub_ref,
                 o_ref, vt_scr, sa_scr, sb_scr, m_scr, acc_scr, corr_scr, *, t, lam_init):
    h = pl.program_id(0)
    nkt = vt_scr.shape[0]
    slope2 = slope_ref[h] * LOG2E
    c_hi = jnp.full((LANES, t), slope2, F32).astype(BF16).astype(F32)
    c_lo = (jnp.full((LANES, t), slope2, F32) - c_hi).astype(BF16).astype(F32)

    ones_row = jnp.where(lax.broadcasted_iota(jnp.int32, (ATT_VROWS - V_DIM, t), 0) == 0, 1.0, 0.0)

    def tr(c, carry):
        vc = v_ref[0, pl.ds(pl.multiple_of(c * t, t), t), :]
        vt_scr[c] = jnp.concatenate([vc.astype(F32).T, ones_row], axis=0).astype(BF16)
        return carry
    lax.fori_loop(0, nkt, tr, 0)

    krel = lax.broadcasted_iota(jnp.int32, (t, t), 0)
    qrel = lax.broadcasted_iota(jnp.int32, (t, t), 1)
    shift = CHUNK.bit_length() - 1
    allowed = (krel >> shift) <= (qrel >> shift)
    c_mxu = (c_hi + c_lo)[0:1, 0:1]
    corr_scr[...] = jnp.where(allowed, slope2 * (qrel - jnp.abs(qrel - krel)).astype(F32)
                              - c_mxu * krel.astype(F32), NEG_BIG)

    step_dec = slope2 * t
    row = lax.broadcasted_iota(jnp.int32, (LANES, t), 0)
    qfeat = jnp.where(row == 0, c_hi, jnp.where(row == 1, ATT_POS * c_hi,
                      jnp.where(row == 2, c_lo, jnp.where(row == 3, ATT_POS * c_lo, 0.0))))
    lam = (jnp.exp(jnp.sum(lq1_ref[...] * lk1_ref[...], axis=-1, keepdims=True))
           - jnp.exp(jnp.sum(lq2_ref[...] * lk2_ref[...], axis=-1, keepdims=True)) + lam_init)

    def colmax(s):
        while s.shape[0] > 8:
            half = s.shape[0] // 2
            s = jnp.maximum(s[:half], s[half:])
        return jnp.max(s, axis=0, keepdims=True)

    def query_tile(qi, carry):
        rows = pl.ds(pl.multiple_of(qi * t, t), t)
        qt = (q_ref[0, rows, :].astype(F32) * (LOG2E * QK_DIM ** -0.5)).T
        qa = jnp.concatenate([jnp.where(row < QK_DIM, qt, 0.0), qfeat], axis=0)
        qb = jnp.concatenate([jnp.where(row >= QK_DIM, qt, 0.0), qfeat], axis=0)
        qts = jnp.concatenate([qa, qb], axis=1).astype(BF16)

        krow = lax.broadcasted_iota(jnp.int32, (t, LANES), 0)
        klane = lax.broadcasted_iota(jnp.int32, (t, LANES), 1)
        pbits = ATT_POS.bit_length() - 1
        kmod = (krow & (ATT_POS - 1)).astype(F32)
        kdiv = (krow >> pbits).astype(F32)
        kfeat = jnp.where((klane == 0) | (klane == 2), kmod,
                          jnp.where((klane == 1) | (klane == 3), kdiv, 0.0)).astype(BF16)

        m_scr[...] = jnp.full(m_scr.shape, NEG_BIG, F32)
        acc_scr[...] = jnp.zeros(acc_scr.shape, F32)

        def scores(j, s_ref):
            kj = k_ref[0, pl.ds(pl.multiple_of(j * t, t), t), :]
            s_ref[...] = jnp.dot(jnp.concatenate([kj, kfeat], axis=1), qts, preferred_element_type=F32)

        def softmax_pv(j, s_ref, corr):
            s = s_ref[...]
            if corr is not None:
                s = s + corr
            m_old = m_scr[...] - step_dec
            m_new = jnp.maximum(m_old, colmax(s))
            alpha = jnp.exp2(m_old - m_new)
            p = jnp.exp2(s - m_new).astype(BF16)
            pv = jnp.dot(vt_scr[j], p, preferred_element_type=F32)
            acc_scr[...] = alpha * acc_scr[...] + pv
            m_scr[...] = m_new

        def near(s_ref):
            corr = corr_scr[...]
            softmax_pv(qi, s_ref, jnp.concatenate([corr, corr], axis=1))

        scores(0, sa_scr)

        def pair(i, c):
            scores(2 * i + 1, sb_scr)
            softmax_pv(2 * i, sa_scr, None)
            scores(2 * i + 2, sa_scr)
            softmax_pv(2 * i + 1, sb_scr, None)
            return c

        def quad(i, c):
            return pair(2 * i + 1, pair(2 * i, c))

        def octo(i, c):
            return quad(2 * i + 1, quad(2 * i, c))

        lax.fori_loop(0, qi // 8, octo, 0)
        lax.fori_loop(2 * (qi // 8), qi // 4, quad, 0)
        lax.fori_loop(2 * (qi // 4), qi // 2, pair, 0)

        @pl.when(qi % 2 == 1)
        def _():
            scores(qi, sb_scr)
            softmax_pv(qi - 1, sa_scr, None)
            near(sb_scr)

        @pl.when(qi % 2 == 0)
        def _():
            near(sa_scr)

        o1 = acc_scr[0:V_DIM, 0:t] / acc_scr[V_DIM:V_DIM + 1, 0:t]
        o2 = acc_scr[0:V_DIM, t:2 * t] / acc_scr[V_DIM:V_DIM + 1, t:2 * t]
        ot = o1 - lam * o2
        ot = ot * lax.rsqrt(jnp.mean(ot * ot, axis=0, keepdims=True) + SUBLN_EPS)
        o_ref[0, rows, :] = (ot.T * gsub_ref[...] * (1.0 - lam_init)).astype(BF16)
        return carry

    lax.fori_loop(0, nkt, query_tile, 0)


def _attn(slopes, lq1, lk1, lq2, lk2, proj, gsub, lam_init, t=512):
    s = proj.shape[1]
    smem = pl.BlockSpec(memory_space=pltpu.SMEM)
    vec = lambda n: pl.BlockSpec((1, n), lambda h: (0, 0))
    head = lambda base: pl.BlockSpec((1, s, LANES), lambda h, base=base: (base + h, 0, 0))
    return pl.pallas_call(
        functools.partial(_attn_kernel, t=t, lam_init=lam_init),
        out_shape=jax.ShapeDtypeStruct((N_HEADS, s, V_DIM), BF16),
        grid=(N_HEADS,),
        in_specs=[smem, vec(QK_DIM), vec(QK_DIM), vec(QK_DIM), vec(QK_DIM),
                  head(0), head(N_HEADS), head(2 * N_HEADS), vec(V_DIM)],
        out_specs=pl.BlockSpec((1, s, V_DIM), lambda h: (h, 0, 0)),
        scratch_shapes=[pltpu.VMEM((s // t, ATT_VROWS, t), BF16),
                        pltpu.VMEM((t, 2 * t), F32), pltpu.VMEM((t, 2 * t), F32),
                        pltpu.VMEM((1, 2 * t), F32),
                        pltpu.VMEM((ATT_VROWS, 2 * t), F32),
                        pltpu.VMEM((t, t), F32)],
        compiler_params=pltpu.CompilerParams(dimension_semantics=("arbitrary",),
                                             vmem_limit_bytes=VMEM_LIMIT),
        name="attn",
    )(slopes, lq1, lk1, lq2, lk2, proj, proj, proj, gsub)


def _route(logits):
    lane = lax.broadcasted_iota(jnp.int32, logits.shape, 1).astype(F32)
    far = float(4 * LANES)
    gl = jnp.where(lane < N_GROUPS, logits, NEG_BIG)
    gmax = jnp.max(gl, axis=-1, keepdims=True)
    g_w = 1.0 / jnp.sum(jnp.exp(gl - gmax), axis=-1, keepdims=True)
    g_idx = jnp.min(jnp.where(gl == gmax, lane, far), axis=-1, keepdims=True)
    lo = N_GROUPS + EXPERTS_PER_GROUP * g_idx
    el = jnp.where((lane >= lo) & (lane < lo + EXPERTS_PER_GROUP), logits, NEG_BIG)
    e1 = jnp.max(el, axis=-1, keepdims=True)
    i1 = jnp.min(jnp.where(el == e1, lane, far), axis=-1, keepdims=True)
    el2 = jnp.where(lane == i1, NEG_BIG, el)
    e2 = jnp.max(el2, axis=-1, keepdims=True)
    i2 = jnp.min(jnp.where(el2 == e2, lane, far), axis=-1, keepdims=True)
    r = jnp.exp(e2 - e1)
    w1 = 1.0 / (1.0 + r)
    w2 = r * w1
    out = jnp.where(lane == 0.0, i1 - N_GROUPS, 0.0)
    out = jnp.where(lane == 1.0, i2 - N_GROUPS, out)
    out = jnp.where(lane == 2.0, g_w * w1, out)
    out = jnp.where(lane == 3.0, g_w * w2, out)
    return out


def _outproj_kernel(ya_ref, b_ref, c_ref, u_ref, x_ref, wconv_ref, wout_ref, gpost_ref, gt_ref,
                    gpre_ref, sc_ref, sh_ref, wr_ref, br_ref,
                    x1_ref, h2_ref, rinfo_ref, cnt_ref, zbuf, ybuf, *, tm, nc):
    i = pl.program_id(0)
    halo = 8

    @pl.when(i == 0)
    def _():
        zbuf[0:halo, :] = jnp.zeros((halo, zbuf.shape[1]), F32)

    @pl.when(i > 0)
    def _():
        zbuf[0:halo, :] = zbuf[tm:tm + halo, :]

    for c in range(nc):
        cols = slice(c * LANES, (c + 1) * LANES)
        zbuf[halo:halo + tm, cols] = c_ref[c].astype(F32) * u_ref[c].astype(F32)
    d_attn = ya_ref.shape[0] * LANES
    nh = 2
    hm = tm // nh
    counts = jnp.zeros((1, LANES), F32)
    for part in range(nh):
        r0 = part * hm
        rows = slice(r0, r0 + hm)
        for c in range(ya_ref.shape[0]):
            ybuf[rows, c * LANES:(c + 1) * LANES] = ya_ref[c, rows, :]
        for c in range(nc):
            cols = slice(c * LANES, (c + 1) * LANES)
            conv = (wconv_ref[0:1, cols] * zbuf[r0 + halo - 2:r0 + halo - 2 + hm, cols]
                    + wconv_ref[1:2, cols] * zbuf[r0 + halo - 1:r0 + halo - 1 + hm, cols]
                    + wconv_ref[2:3, cols] * zbuf[r0 + halo:r0 + halo + hm, cols])
            ybuf[rows, d_attn + c * LANES:d_attn + (c + 1) * LANES] = (
                b_ref[c, rows, :].astype(F32) * conv).astype(BF16)

        y = jnp.dot(ybuf[rows, :], wout_ref[...], preferred_element_type=F32)
        x1 = x_ref[rows, :] + gt_ref[...] * (_rms(y, NORM_EPS) * gpost_ref[...])
        x1_ref[rows, :] = x1
        h2 = (_rms(x1, NORM_EPS) * gpre_ref[...]) * (1.0 + sc_ref[...]) + sh_ref[...]
        h2_ref[rows, :] = h2
        logits = jnp.dot(h2.astype(BF16), wr_ref[...], preferred_element_type=F32) + br_ref[...]
        routed = _route(logits)
        rinfo_ref[rows, :] = routed
        lane = lax.broadcasted_iota(jnp.int32, routed.shape, 1).astype(F32)
        picked = jnp.where(lane == routed[:, 0:1], 1.0, 0.0) + jnp.where(lane == routed[:, 1:2], 1.0, 0.0)
        counts = counts + jnp.sum(picked, axis=0, keepdims=True)
    cnt_ref[0] = counts


def _outproj(y_attn, proj, x2, w_conv, wout_bf, gpost, gt, gpre, sc, sh, wr_bf, br, tm=512):
    s, d = x2.shape
    nh = y_attn.shape[0]
    d_attn = nh * LANES
    nc = (d - d_attn) // LANES
    base = 3 * d_attn // LANES
    row = lambda n: pl.BlockSpec((1, n), lambda i: (0, 0))
    blk = lambda g: pl.BlockSpec((nc, tm, LANES), lambda i, g=g: (base // nc + g, i, 0))
    return pl.pallas_call(
        functools.partial(_outproj_kernel, tm=tm, nc=nc),
        out_shape=(jax.ShapeDtypeStruct((s, d), F32),
                   jax.ShapeDtypeStruct((s, d), F32),
                   jax.ShapeDtypeStruct((s, LANES), F32),
                   jax.ShapeDtypeStruct((s // tm, 1, LANES), F32)),
        grid=(s // tm,),
        in_specs=[pl.BlockSpec((nh, tm, LANES), lambda i: (0, i, 0)),
                  blk(0), blk(1), blk(2),
                  pl.BlockSpec((tm, d), lambda i: (i, 0)),
                  pl.BlockSpec((CONV_WIDTH, d - d_attn), lambda i: (0, 0)),
                  pl.BlockSpec((d, d), lambda i: (0, 0)),
                  row(d), row(d), row(d), row(d), row(d),
                  pl.BlockSpec((d, LANES), lambda i: (0, 0)),
                  row(LANES)],
        out_specs=(pl.BlockSpec((tm, d), lambda i: (i, 0)),
                   pl.BlockSpec((tm, d), lambda i: (i, 0)),
                   pl.BlockSpec((tm, LANES), lambda i: (i, 0)),
                   pl.BlockSpec((1, 1, LANES), lambda i: (i, 0, 0))),
        scratch_shapes=[pltpu.VMEM((tm + 8, d - d_attn), F32), pltpu.VMEM((tm, d), BF16)],
        compiler_params=pltpu.CompilerParams(dimension_semantics=("arbitrary",),
                                             vmem_limit_bytes=VMEM_LIMIT),
        name="outproj",
    )(y_attn, proj, proj, proj, x2, w_conv, wout_bf, gpost, gt, gpre, sc, sh, wr_bf, br)


MOE_TILE = 256
EXPERT_TILES_PER_STEP = 2


def _moe_num_tiles(s):
    return (2 * s) // MOE_TILE + N_EXPERTS


def _plan_kernel(rinfo_ref, cnt_ref, pos_ref, tinfo_ref, base_scr, *, tm, ntm):
    i = pl.program_id(0)
    r = rinfo_ref[...]
    lane = lax.broadcasted_iota(jnp.int32, r.shape, 1).astype(F32)
    oh0 = jnp.where(lane == r[:, 0:1], 1.0, 0.0)
    oh1 = jnp.where(lane == r[:, 1:2], 1.0, 0.0)
    both = oh0 + oh1
    tile_cnt = jnp.sum(both, axis=0, keepdims=True)

    @pl.when(i == 0)
    def _():
        total = jnp.sum(cnt_ref[...], axis=0)
        ktiles = jnp.floor((total + (MOE_TILE - 1)) * (1.0 / MOE_TILE))
        rr = lax.broadcasted_iota(jnp.int32, (LANES, LANES), 0)
        cc = lax.broadcasted_iota(jnp.int32, (LANES, LANES), 1)
        upper = jnp.where(rr < cc, 1.0, 0.0).astype(BF16)
        first = jnp.dot(jnp.broadcast_to(ktiles, (8, LANES)).astype(BF16), upper,
                        preferred_element_type=F32)[0:1, :]
        base_scr[...] = first * MOE_TILE
        ti = lax.broadcasted_iota(jnp.int32, (ntm, LANES), 0).astype(F32)
        el = lax.broadcasted_iota(jnp.int32, (ntm, LANES), 1).astype(F32)
        owned = jnp.where(ti >= first, jnp.where(ti < first + ktiles, 1.0, 0.0), 0.0)
        texp = jnp.sum(owned * el, axis=1, keepdims=True)
        nact = jnp.sum(ktiles, axis=1, keepdims=True)
        tinfo_ref[...] = jnp.where(el == 0.0, texp, jnp.where(el == 1.0, nact, 0.0))

    rr = lax.broadcasted_iota(jnp.int32, (tm, tm), 0)
    cc = lax.broadcasted_iota(jnp.int32, (tm, tm), 1)
    lower = jnp.where(cc < rr, 1.0, 0.0).astype(BF16)
    before = jnp.dot(lower, both.astype(BF16), preferred_element_type=F32)
    val = before + base_scr[...]
    p0 = jnp.sum(oh0 * val, axis=1, keepdims=True)
    p1 = jnp.sum(oh1 * val, axis=1, keepdims=True)
    pos_ref[...] = jnp.where(lane == 0.0, p0, jnp.where(lane == 1.0, p1, 0.0))
    base_scr[...] += tile_cnt


def _plan(rinfo, cnt, tm=512):
    s = rinfo.shape[0]
    ntm = _moe_num_tiles(s)
    return pl.pallas_call(
        functools.partial(_plan_kernel, tm=tm, ntm=ntm),
        out_shape=(jax.ShapeDtypeStruct((s, LANES), F32), jax.ShapeDtypeStruct((ntm, LANES), F32)),
        grid=(s // tm,),
        in_specs=[pl.BlockSpec((tm, LANES), lambda i: (i, 0)),
                  pl.BlockSpec(cnt.shape, lambda i: (0, 0, 0))],
        out_specs=(pl.BlockSpec((tm, LANES), lambda i: (i, 0)),
                   pl.BlockSpec((ntm, LANES), lambda i: (0, 0))),
        scratch_shapes=[pltpu.VMEM((1, LANES), F32)],
        compiler_params=pltpu.CompilerParams(dimension_semantics=("arbitrary",),
                                             vmem_limit_bytes=VMEM_LIMIT),
        name="plan",
    )(rinfo, cnt)


def _dispatch_kernel(pos_ref, texp_ref, nact_ref, h2_ref, xs_ref, zero_scr, sem, zsem, *, tm):
    i = pl.program_id(0)

    @pl.when(i == 0)
    def _():
        zero_scr[...] = jnp.zeros_like(zero_scr)
        nact = nact_ref[0]
        ntm = xs_ref.shape[0] // MOE_TILE

        def is_last(t):
            nxt = texp_ref[jnp.minimum(t + 1, nact - 1)]
            return (t >= nact - 1) | (nxt != texp_ref[jnp.minimum(t, nact - 1)])

        def zstart(t, carry):
            @pl.when(is_last(t))
            def _():
                pltpu.make_async_copy(zero_scr, xs_ref.at[pl.ds(pl.multiple_of(t * MOE_TILE, MOE_TILE), MOE_TILE), :],
                                      zsem).start()
            return carry

        def zwait(t, carry):
            @pl.when(is_last(t))
            def _():
                pltpu.make_async_copy(zero_scr, xs_ref.at[pl.ds(0, MOE_TILE), :], zsem).wait()
            return carry

        lax.fori_loop(0, ntm, zstart, 0)
        lax.fori_loop(0, ntm, zwait, 0)

    def start(r, carry):
        tok = i * tm + r
        for slot in range(2):
            p = pos_ref[2 * tok + slot]
            pltpu.make_async_copy(h2_ref.at[pl.ds(r, 1), :], xs_ref.at[pl.ds(p, 1), :], sem).start()
        return carry

    lax.fori_loop(0, tm, start, 0, unroll=8)
    for slot in range(2):
        pltpu.make_async_copy(h2_ref, xs_ref.at[pl.ds(0, tm), :], sem).wait()


def _dispatch(pos_flat, texp, nact, h2, tm=1024):
    s, d = h2.shape
    ntm = _moe_num_tiles(s)
    return pl.pallas_call(
        functools.partial(_dispatch_kernel, tm=tm),
        out_shape=jax.ShapeDtypeStruct((ntm * MOE_TILE, d), F32),
        grid_spec=pltpu.PrefetchScalarGridSpec(
            num_scalar_prefetch=3,
            grid=(s // tm,),
            in_specs=[pl.BlockSpec((tm, d), lambda i, *_: (i, 0))],
            out_specs=pl.BlockSpec(memory_space=pl.ANY),
            scratch_shapes=[pltpu.VMEM((MOE_TILE, d), F32), pltpu.SemaphoreType.DMA, pltpu.SemaphoreType.DMA]),
        compiler_params=pltpu.CompilerParams(dimension_semantics=("arbitrary",),
                                             vmem_limit_bytes=VMEM_LIMIT),
        name="dispatch",
    )(pos_flat, texp, nact, h2)


def _experts_kernel(texp_ref, nact_ref, xs_ref, wg_hbm, wu_hbm, wd_hbm, ys_ref,
                    wg_f32, wu_f32, wd_f32, ord_scr, sem):
    nact = nact_ref[0]

    def weight_copies(expert, slot):
        return (pltpu.make_async_copy(wg_hbm.at[expert], wg_f32.at[slot], sem.at[slot]),
                pltpu.make_async_copy(wu_hbm.at[expert], wu_f32.at[slot], sem.at[slot]),
                pltpu.make_async_copy(wd_hbm.at[expert], wd_f32.at[slot], sem.at[slot]))

    def one_tile(i, rows):
        j = jnp.minimum(i, nact - 1)
        e = texp_ref[j]
        fresh = (i == 0) | (e != texp_ref[jnp.maximum(j - 1, 0)])

        @pl.when(i == 0)
        def _():
            ord_scr[0] = 0
            for c in weight_copies(e, 0):
                c.start()

        @pl.when((i < nact) & fresh)
        def _():
            slot = ord_scr[0] % 2
            for c in weight_copies(e, slot):
                c.wait()
            nxt = lax.while_loop(lambda t: (t < nact) & (texp_ref[jnp.minimum(t, nact - 1)] == e),
                                 lambda t: t + 1, i + 1)

            @pl.when(nxt < nact)
            def _():
                for c in weight_copies(texp_ref[nxt], 1 - slot):
                    c.start(priority=1)
            ord_scr[0] = ord_scr[0] + 1

        @pl.when(i < nact)
        def _():
            slot = (ord_scr[0] + 1) % 2
            x = xs_ref[rows, :].astype(BF16)
            hg = jnp.dot(x, wg_f32[slot].astype(BF16), preferred_element_type=F32)
            hu = jnp.dot(x, wu_f32[slot].astype(BF16), preferred_element_type=F32)
            a = (hg * _sigmoid(hg)) * hu
            ys_ref[rows, :] = jnp.dot(a.astype(BF16), wd_f32[slot].astype(BF16), preferred_element_type=F32)

        @pl.when(i >= nact)
        def _():
            ys_ref[rows, :] = jnp.zeros((MOE_TILE, ys_ref.shape[1]), F32)

    for sub in range(EXPERT_TILES_PER_STEP):
        one_tile(pl.program_id(0) * EXPERT_TILES_PER_STEP + sub, slice(sub * MOE_TILE, (sub + 1) * MOE_TILE))


def _experts(texp, nact, xs, w_gate, w_up, w_down):
    p, d = xs.shape
    ntm = p // MOE_TILE
    _, _, f = w_gate.shape

    tps = EXPERT_TILES_PER_STEP
    assert ntm % tps == 0

    def tile(i, texp_ref, nact_ref):
        return (jnp.minimum(i, (nact_ref[0] - 1) // tps), 0)

    hbm = pl.BlockSpec(memory_space=pl.ANY)
    return pl.pallas_call(
        _experts_kernel,
        out_shape=jax.ShapeDtypeStruct((p, d), F32),
        grid_spec=pltpu.PrefetchScalarGridSpec(
            num_scalar_prefetch=2,
            grid=(ntm // tps,),
            in_specs=[pl.BlockSpec((tps * MOE_TILE, d), tile), hbm, hbm, hbm],
            out_specs=pl.BlockSpec((tps * MOE_TILE, d), lambda i, *_: (i, 0)),
            scratch_shapes=[pltpu.VMEM((2, d, f), F32), pltpu.VMEM((2, d, f), F32), pltpu.VMEM((2, f, d), F32),
                            pltpu.SMEM((1,), jnp.int32), pltpu.SemaphoreType.DMA((2,))]),
        compiler_params=pltpu.CompilerParams(dimension_semantics=("arbitrary",),
                                             vmem_limit_bytes=VMEM_LIMIT),
        name="experts",
    )(texp, nact, xs, w_gate, w_up, w_down)


def _combine_kernel(pos_ref, ys_ref, rinfo_ref, x1_ref, gpost_ref, gt_ref, o_ref, ybuf, sem, *, tm):
    i = pl.program_id(0)

    def gather(tile, b):
        def start(r, carry):
            tok = tile * tm + r
            for slot in range(2):
                p = pos_ref[2 * tok + slot]
                pltpu.make_async_copy(ys_ref.at[pl.ds(p, 1), :], ybuf.at[b, slot, pl.ds(r, 1), :],
                                      sem.at[b]).start()
            return carry
        lax.fori_loop(0, tm, start, 0, unroll=8)

    @pl.when(i == 0)
    def _():
        gather(0, 0)

    @pl.when(i + 1 < pl.num_programs(0))
    def _():
        gather(i + 1, (i + 1) % 2)

    b = i % 2
    for slot in range(2):
        pltpu.make_async_copy(ys_ref.at[pl.ds(0, tm), :], ybuf.at[b, slot], sem.at[b]).wait()
    r = rinfo_ref[...]
    y = r[:, 2:3] * ybuf[b, 0] + r[:, 3:4] * ybuf[b, 1]
    o_ref[...] = x1_ref[...] + gt_ref[...] * (_rms(y, NORM_EPS) * gpost_ref[...])


def _combine(pos_flat, ys, rinfo, x1, gpost, gt, tm=512):
    s, d = x1.shape
    row = pl.BlockSpec((1, d), lambda i, *_: (0, 0))
    return pl.pallas_call(
        functools.partial(_combine_kernel, tm=tm),
        out_shape=jax.ShapeDtypeStruct((s, d), F32),
        grid_spec=pltpu.PrefetchScalarGridSpec(
            num_scalar_prefetch=1,
            grid=(s // tm,),
            in_specs=[pl.BlockSpec(memory_space=pl.ANY),
                      pl.BlockSpec((tm, LANES), lambda i, *_: (i, 0)),
                      pl.BlockSpec((tm, d), lambda i, *_: (i, 0)),
                      row, row],
            out_specs=pl.BlockSpec((tm, d), lambda i, *_: (i, 0)),
            scratch_shapes=[pltpu.VMEM((2, 2, tm, d), F32), pltpu.SemaphoreType.DMA((2,))]),
        compiler_params=pltpu.CompilerParams(dimension_semantics=("arbitrary",),
                                             vmem_limit_bytes=VMEM_LIMIT),
        name="combine",
    )(pos_flat, ys, rinfo, x1, gpost, gt)


def _moe(h2, rinfo, cnt, w_gate, w_up, w_down, x1, gpost, gt):
    pos, tinfo = _plan(rinfo, cnt)
    pos_flat = pos[:, 0:2].astype(jnp.int32).reshape(-1)
    texp = tinfo[:, 0].astype(jnp.int32)
    nact = tinfo[0:1, 1].astype(jnp.int32)
    xs = _dispatch(pos_flat, texp, nact, h2)
    ys = _experts(texp, nact, xs, w_gate, w_up, w_down)
    return _combine(pos_flat, ys, rinfo, x1, gpost, gt)


def kernel(x, c, w_ada, b_ada, g_pre1, w_in, lam_q1, lam_k1, lam_q2, lam_k2, g_subln, w_conv, w_out,
           g_post1, g_pre2, w_router_g, b_router_g, w_router_e, b_router_e, w_gate, w_up, w_down, g_post2):
    b, s, d = x.shape
    assert b == 1
    depth = w_ada.shape[0]
    slopes = jnp.asarray([2.0 ** (-8.0 * (i + 1) / N_HEADS) for i in range(N_HEADS)], F32)
    x2 = x.reshape(s, d)
    for l in range(depth):
        lam_init = 0.8 - 0.6 * math.exp(-0.3 * l)
        mod = _ada(c.reshape(d, 1), w_ada[l], b_ada[l].reshape(1, -1))
        sh1, sc1, gt1, sh2, sc2, gt2 = [mod[:, k * d:(k + 1) * d] for k in range(6)]
        proj = _inproj(x2, g_pre1[l].reshape(1, d), sc1, sh1, w_in[l])
        y_attn = _attn(slopes, lam_q1[l].reshape(1, -1), lam_k1[l].reshape(1, -1),
                       lam_q2[l].reshape(1, -1), lam_k2[l].reshape(1, -1), proj,
                       g_subln[l].reshape(1, -1), lam_init)
        wr = jnp.concatenate([w_router_g[l], w_router_e[l].reshape(d, N_EXPERTS),
                              jnp.zeros((d, LANES - N_GROUPS - N_EXPERTS), F32)], axis=1).astype(BF16)
        br = jnp.concatenate([b_router_g[l], b_router_e[l].reshape(N_EXPERTS),
                              jnp.zeros((LANES - N_GROUPS - N_EXPERTS,), F32)]).reshape(1, LANES)
        x1, h2, rinfo, cnt = _outproj(y_attn, proj, x2, w_conv[l], w_out[l].astype(BF16),
                                 g_post1[l].reshape(1, d), gt1, g_pre2[l].reshape(1, d), sc2, sh2, wr, br)
        x2 = _moe(h2, rinfo, cnt, w_gate[l], w_up[l], w_down[l], x1, g_post2[l].reshape(1, d), gt2)
    return x2.reshape(b, s, d)
```

```python
import functools
import math

import jax
import jax.numpy as jnp
from jax import lax
from jax.experimental import pallas as pl
from jax.experimental.pallas import tpu as pltpu

F32 = jnp.float32
BF16 = jnp.bfloat16

LANES = 128
CHUNK = 64
N_HEADS = 8
QK_DIM = 64
V_DIM = 128
CONV_WIDTH = 3
N_GROUPS = 4
EXPERTS_PER_GROUP = 8
N_EXPERTS = N_GROUPS * EXPERTS_PER_GROUP
NORM_EPS = 1e-6
SUBLN_EPS = 1e-5
NEG_BIG = -1e30
LOG2E = 1.4426950408889634
VMEM_LIMIT = 56 * 1024 * 1024


def _rms(x, eps):
    return x * lax.rsqrt(jnp.mean(x * x, axis=-1, keepdims=True) + eps)


def _sigmoid(x):
    return 1.0 / (1.0 + jnp.exp(-x))


def _ada_kernel(c_ref, w_ref, b_ref, o_ref):
    c = c_ref[...]
    s = c * _sigmoid(c)
    o_ref[...] = jnp.sum(w_ref[...] * s, axis=0, keepdims=True) + b_ref[...]


def _ada(c_col, w_ada, b_ada, tn=1024):
    d, n = w_ada.shape
    return pl.pallas_call(
        _ada_kernel,
        out_shape=jax.ShapeDtypeStruct((1, n), F32),
        grid=(n // tn,),
        in_specs=[pl.BlockSpec((d, 1), lambda j: (0, 0)),
                  pl.BlockSpec((d, tn), lambda j: (0, j)),
                  pl.BlockSpec((1, tn), lambda j: (0, j))],
        out_specs=pl.BlockSpec((1, tn), lambda j: (0, j)),
        compiler_params=pltpu.CompilerParams(dimension_semantics=("arbitrary",),
                                             vmem_limit_bytes=VMEM_LIMIT),
        name="ada",
    )(c_col, w_ada, b_ada)


def _inproj_kernel(x_ref, g_ref, sc_ref, sh_ref, w_ref, o_ref, h_scr, *, cn):
    @pl.when(pl.program_id(1) == 0)
    def _():
        rows = 16

        scale = g_ref[...] * (1.0 + sc_ref[...])

        def norm_rows(r, carry):
            sl = pl.ds(pl.multiple_of(r * rows, rows), rows)
            h_scr[sl, :] = (_rms(x_ref[sl, :], NORM_EPS) * scale + sh_ref[...]).astype(BF16)
            return carry
        lax.fori_loop(0, x_ref.shape[0] // rows, norm_rows, 0, unroll=8)

    acc = jnp.dot(h_scr[...], w_ref[...].astype(BF16), preferred_element_type=F32)
    for c in range(cn):
        o_ref[c] = acc[:, c * LANES:(c + 1) * LANES].astype(BF16)


def _inproj(x2, g, sc, sh, w, tm=1024, cn=12):
    s, d = x2.shape
    n = w.shape[1]
    nchunks = n // LANES
    row = lambda i, j: (0, 0)
    return pl.pallas_call(
        functools.partial(_inproj_kernel, cn=cn),
        out_shape=jax.ShapeDtypeStruct((nchunks, s, LANES), BF16),
        grid=(s // tm, nchunks // cn),
        in_specs=[pl.BlockSpec((tm, d), lambda i, j: (i, 0)),
                  pl.BlockSpec((1, d), row),
                  pl.BlockSpec((1, d), row),
                  pl.BlockSpec((1, d), row),
                  pl.BlockSpec((d, cn * LANES), lambda i, j: (0, j))],
        out_specs=pl.BlockSpec((cn, tm, LANES), lambda i, j: (j, i, 0)),
        scratch_shapes=[pltpu.VMEM((tm, d), BF16)],
        compiler_params=pltpu.CompilerParams(dimension_semantics=("arbitrary", "arbitrary"),
                                             vmem_limit_bytes=VMEM_LIMIT),
        name="inproj",
    )(x2, g, sc, sh, w)


ATT_POS = 256
ATT_VROWS = 144


def _attn_kernel(slope_ref, lq1_ref, lk1_ref, lq2_ref, lk2_ref, q_ref, k_ref, v_ref, gsub_ref,
                 o_ref, vt_scr, sa_scr, sb_scr, m_scr, acc_scr, corr_scr, *, t, lam_init):
    h = pl.program_id(0)
    nkt = vt_scr.shape[0]
    slope2 = slope_ref[h] * LOG2E
    c_hi = jnp.full((LANES, t), slope2, F32).astype(BF16).astype(F32)
    c_lo = (jnp.full((LANES, t), slope2, F32) - c_hi).astype(BF16).astype(F32)

    ones_row = jnp.where(lax.broadcasted_iota(jnp.int32, (ATT_VROWS - V_DIM, t), 0) == 0, 1.0, 0.0)

    def tr(c, carry):
        vc = v_ref[0, pl.ds(pl.multiple_of(c * t, t), t), :]
        vt_scr[c] = jnp.concatenate([vc.astype(F32).T, ones_row], axis=0).astype(BF16)
        return carry
    lax.fori_loop(0, nkt, tr, 0)

    krel = lax.broadcasted_iota(jnp.int32, (t, t), 0)
    qrel = lax.broadcasted_iota(jnp.int32, (t, t), 1)
    shift = CHUNK.bit_length() - 1
    allowed = (krel >> shift) <= (qrel >> shift)
    c_mxu = (c_hi + c_lo)[0:1, 0:1]
    corr_scr[...] = jnp.where(allowed, slope2 * (qrel - jnp.abs(qrel - krel)).astype(F32)
                              - c_mxu * krel.astype(F32), NEG_BIG)

    step_dec = slope2 * t
    row = lax.broadcasted_iota(jnp.int32, (LANES, t), 0)
    qfeat = jnp.where(row == 0, c_hi, jnp.where(row == 1, ATT_POS * c_hi,
                      jnp.where(row == 2, c_lo, jnp.where(row == 3, ATT_POS * c_lo, 0.0))))
    lam = (jnp.exp(jnp.sum(lq1_ref[...] * lk1_ref[...], axis=-1, keepdims=True))
           - jnp.exp(jnp.sum(lq2_ref[...] * lk2_ref[...], axis=-1, keepdims=True)) + lam_init)

    def colmax(s):
        while s.shape[0] > 8:
            half = s.shape[0] // 2
            s = jnp.maximum(s[:half], s[half:])
        return jnp.max(s, axis=0, keepdims=True)

    def query_tile(qi, carry):
        rows = pl.ds(pl.multiple_of(qi * t, t), t)
        qt = (q_ref[0, rows, :].astype(F32) * (LOG2E * QK_DIM ** -0.5)).T
        qa = jnp.concatenate([jnp.where(row < QK_DIM, qt, 0.0), qfeat], axis=0)
        qb = jnp.concatenate([jnp.where(row >= QK_DIM, qt, 0.0), qfeat], axis=0)
        qts = jnp.concatenate([qa, qb], axis=1).astype(BF16)

        krow = lax.broadcasted_iota(jnp.int32, (t, LANES), 0)
        klane = lax.broadcasted_iota(jnp.int32, (t, LANES), 1)
        pbits = ATT_POS.bit_length() - 1
        kmod = (krow & (ATT_POS - 1)).astype(F32)
        kdiv = (krow >> pbits).astype(F32)
        kfeat = jnp.where((klane == 0) | (klane == 2), kmod,
                          jnp.where((klane == 1) | (klane == 3), kdiv, 0.0)).astype(BF16)

        m_scr[...] = jnp.full(m_scr.shape, NEG_BIG, F32)
        acc_scr[...] = jnp.zeros(acc_scr.shape, F32)

        def scores(j, s_ref):
            kj = k_ref[0, pl.ds(pl.multiple_of(j * t, t), t), :]
            s_ref[...] = jnp.dot(jnp.concatenate([kj, kfeat], axis=1), qts, preferred_element_type=F32)

        def softmax_pv(j, s_ref, corr):
            s = s_ref[...]
            if corr is not None:
                s = s + corr
            m_old = m_scr[...] - step_dec
            m_new = jnp.maximum(m_old, colmax(s))
            alpha = jnp.exp2(m_old - m_new)
            p = jnp.exp2(s - m_new).astype(BF16)
            pv = jnp.dot(vt_scr[j], p, preferred_element_type=F32)
            acc_scr[...] = alpha * acc_scr[...] + pv
            m_scr[...] = m_new

        def near(s_ref):
            corr = corr_scr[...]
            softmax_pv(qi, s_ref, jnp.concatenate([corr, corr], axis=1))

        scores(0, sa_scr)

        def pair(i, c):
            scores(2 * i + 1, sb_scr)
            softmax_pv(2 * i, sa_scr, None)
            scores(2 * i + 2, sa_scr)
            softmax_pv(2 * i + 1, sb_scr, None)
            return c

        def quad(i, c):
            return pair(2 * i + 1, pair(2 * i, c))

        def octo(i, c):
            return quad(2 * i + 1, quad(2 * i, c))

        lax.fori_loop(0, qi // 8, octo, 0)
        lax.fori_loop(2 * (qi // 8), qi // 4, quad, 0)
        lax.fori_loop(2 * (qi // 4), qi // 2, pair, 0)

        @pl.when(qi % 2 == 1)
        def _():
            scores(qi, sb_scr)
            softmax_pv(qi - 1, sa_scr, None)
            near(sb_scr)

        @pl.when(qi % 2 == 0)
        def _():
            near(sa_scr)

        o1 = acc_scr[0:V_DIM, 0:t] / acc_scr[V_DIM:V_DIM + 1, 0:t]
        o2 = acc_scr[0:V_DIM, t:2 * t] / acc_scr[V_DIM:V_DIM + 1, t:2 * t]
        ot = o1 - lam * o2
        ot = ot * lax.rsqrt(jnp.mean(ot * ot, axis=0, keepdims=True) + SUBLN_EPS)
        o_ref[0, rows, :] = (ot.T * gsub_ref[...] * (1.0 - lam_init)).astype(BF16)
        return carry

    lax.fori_loop(0, nkt, query_tile, 0)


def _attn(slopes, lq1, lk1, lq2, lk2, proj, gsub, lam_init, t=512):
    s = proj.shape[1]
    smem = pl.BlockSpec(memory_space=pltpu.SMEM)
    vec = lambda n: pl.BlockSpec((1, n), lambda h: (0, 0))
    head = lambda base: pl.BlockSpec((1, s, LANES), lambda h, base=base: (base + h, 0, 0))
    return pl.pallas_call(
        functools.partial(_attn_kernel, t=t, lam_init=lam_init),
        out_shape=jax.ShapeDtypeStruct((N_HEADS, s, V_DIM), BF16),
        grid=(N_HEADS,),
        in_specs=[smem, vec(QK_DIM), vec(QK_DIM), vec(QK_DIM), vec(QK_DIM),
                  head(0), head(N_HEADS), head(2 * N_HEADS), vec(V_DIM)],
        out_specs=pl.BlockSpec((1, s, V_DIM), lambda h: (h, 0, 0)),
        scratch_shapes=[pltpu.VMEM((s // t, ATT_VROWS, t), BF16),
                        pltpu.VMEM((t, 2 * t), F32), pltpu.VMEM((t, 2 * t), F32),
                        pltpu.VMEM((1, 2 * t), F32),
                        pltpu.VMEM((ATT_VROWS, 2 * t), F32),
                        pltpu.VMEM((t, t), F32)],
        compiler_params=pltpu.CompilerParams(dimension_semantics=("arbitrary",),
                                             vmem_limit_bytes=VMEM_LIMIT),
        name="attn",
    )(slopes, lq1, lk1, lq2, lk2, proj, proj, proj, gsub)


def _route(logits):
    lane = lax.broadcasted_iota(jnp.int32, logits.shape, 1).astype(F32)
    far = float(4 * LANES)
    gl = jnp.where(lane < N_GROUPS, logits, NEG_BIG)
    gmax = jnp.max(gl, axis=-1, keepdims=True)
    g_w = 1.0 / jnp.sum(jnp.exp(gl - gmax), axis=-1, keepdims=True)
    g_idx = jnp.min(jnp.where(gl == gmax, lane, far), axis=-1, keepdims=True)
    lo = N_GROUPS + EXPERTS_PER_GROUP * g_idx
    el = jnp.where((lane >= lo) & (lane < lo + EXPERTS_PER_GROUP), logits, NEG_BIG)
    e1 = jnp.max(el, axis=-1, keepdims=True)
    i1 = jnp.min(jnp.where(el == e1, lane, far), axis=-1, keepdims=True)
    el2 = jnp.where(lane == i1, NEG_BIG, el)
    e2 = jnp.max(el2, axis=-1, keepdims=True)
    i2 = jnp.min(jnp.where(el2 == e2, lane, far), axis=-1, keepdims=True)
    r = jnp.exp(e2 - e1)
    w1 = 1.0 / (1.0 + r)
    w2 = r * w1
    out = jnp.where(lane == 0.0, i1 - N_GROUPS, 0.0)
    out = jnp.where(lane == 1.0, i2 - N_GROUPS, out)
    out = jnp.where(lane == 2.0, g_w * w1, out)
    out = jnp.where(lane == 3.0, g_w * w2, out)
    return out


def _outproj_kernel(ya_ref, b_ref, c_ref, u_ref, x_ref, wconv_ref, wout_ref, gpost_ref, gt_ref,
                    gpre_ref, sc_ref, sh_ref, wr_ref, br_ref,
                    x1_ref, h2_ref, rinfo_ref, cnt_ref, zbuf, ybuf, *, tm, nc):
    i = pl.program_id(0)
    halo = 8

    @pl.when(i == 0)
    def _():
        zbuf[0:halo, :] = jnp.zeros((halo, zbuf.shape[1]), F32)

    @pl.when(i > 0)
    def _():
        zbuf[0:halo, :] = zbuf[tm:tm + halo, :]

    for c in range(nc):
        cols = slice(c * LANES, (c + 1) * LANES)
        zbuf[halo:halo + tm, cols] = c_ref[c].astype(F32) * u_ref[c].astype(F32)
    d_attn = ya_ref.shape[0] * LANES
    nh = 2
    hm = tm // nh
    counts = jnp.zeros((1, LANES), F32)
    for part in range(nh):
        r0 = part * hm
        rows = slice(r0, r0 + hm)
        for c in range(ya_ref.shape[0]):
            ybuf[rows, c * LANES:(c + 1) * LANES] = ya_ref[c, rows, :]
        for c in range(nc):
            cols = slice(c * LANES, (c + 1) * LANES)
            conv = (wconv_ref[0:1, cols] * zbuf[r0 + halo - 2:r0 + halo - 2 + hm, cols]
                    + wconv_ref[1:2, cols] * zbuf[r0 + halo - 1:r0 + halo - 1 + hm, cols]
                    + wconv_ref[2:3, cols] * zbuf[r0 + halo:r0 + halo + hm, cols])
            ybuf[rows, d_attn + c * LANES:d_attn + (c + 1) * LANES] = (
                b_ref[c, rows, :].astype(F32) * conv).astype(BF16)

        y = jnp.dot(ybuf[rows, :], wout_ref[...], preferred_element_type=F32)
        x1 = x_ref[rows, :] + gt_ref[...] * (_rms(y, NORM_EPS) * gpost_ref[...])
        x1_ref[rows, :] = x1
        h2 = (_rms(x1, NORM_EPS) * gpre_ref[...]) * (1.0 + sc_ref[...]) + sh_ref[...]
        h2_ref[rows, :] = h2
        logits = jnp.dot(h2.astype(BF16), wr_ref[...], preferred_element_type=F32) + br_ref[...]
        routed = _route(logits)
        rinfo_ref[rows, :] = routed
        lane = lax.broadcasted_iota(jnp.int32, routed.shape, 1).astype(F32)
        picked = jnp.where(lane == routed[:, 0:1], 1.0, 0.0) + jnp.where(lane == routed[:, 1:2], 1.0, 0.0)
        counts = counts + jnp.sum(picked, axis=0, keepdims=True)
    cnt_ref[0] = counts


def _outproj(y_attn, proj, x2, w_conv, wout_bf, gpost, gt, gpre, sc, sh, wr_bf, br, tm=512):
    s, d = x2.shape
    nh = y_attn.shape[0]
    d_attn = nh * LANES
    nc = (d - d_attn) // LANES
    base = 3 * d_attn // LANES
    row = lambda n: pl.BlockSpec((1, n), lambda i: (0, 0))
    blk = lambda g: pl.BlockSpec((nc, tm, LANES), lambda i, g=g: (base // nc + g, i, 0))
    return pl.pallas_call(
        functools.partial(_outproj_kernel, tm=tm, nc=nc),
        out_shape=(jax.ShapeDtypeStruct((s, d), F32),
                   jax.ShapeDtypeStruct((s, d), F32),
                   jax.ShapeDtypeStruct((s, LANES), F32),
                   jax.ShapeDtypeStruct((s // tm, 1, LANES), F32)),
        grid=(s // tm,),
        in_specs=[pl.BlockSpec((nh, tm, LANES), lambda i: (0, i, 0)),
                  blk(0), blk(1), blk(2),
                  pl.BlockSpec((tm, d), lambda i: (i, 0)),
                  pl.BlockSpec((CONV_WIDTH, d - d_attn), lambda i: (0, 0)),
                  pl.BlockSpec((d, d), lambda i: (0, 0)),
                  row(d), row(d), row(d), row(d), row(d),
                  pl.BlockSpec((d, LANES), lambda i: (0, 0)),
                  row(LANES)],
        out_specs=(pl.BlockSpec((tm, d), lambda i: (i, 0)),
                   pl.BlockSpec((tm, d), lambda i: (i, 0)),
                   pl.BlockSpec((tm, LANES), lambda i: (i, 0)),
                   pl.BlockSpec((1, 1, LANES), lambda i: (i, 0, 0))),
        scratch_shapes=[pltpu.VMEM((tm + 8, d - d_attn), F32), pltpu.VMEM((tm, d), BF16)],
        compiler_params=pltpu.CompilerParams(dimension_semantics=("arbitrary",),
                                             vmem_limit_bytes=VMEM_LIMIT),
        name="outproj",
    )(y_attn, proj, proj, proj, x2, w_conv, wout_bf, gpost, gt, gpre, sc, sh, wr_bf, br)


MOE_TILE = 256
EXPERT_TILES_PER_STEP = 2


def _moe_num_tiles(s):
    return (2 * s) // MOE_TILE + N_EXPERTS


def _plan_kernel(rinfo_ref, cnt_ref, pos_ref, tinfo_ref, base_scr, *, tm, ntm):
    i = pl.program_id(0)
    r = rinfo_ref[...]
    lane = lax.broadcasted_iota(jnp.int32, r.shape, 1).astype(F32)
    oh0 = jnp.where(lane == r[:, 0:1], 1.0, 0.0)
    oh1 = jnp.where(lane == r[:, 1:2], 1.0, 0.0)
    both = oh0 + oh1
    tile_cnt = jnp.sum(both, axis=0, keepdims=True)

    @pl.when(i == 0)
    def _():
        total = jnp.sum(cnt_ref[...], axis=0)
        ktiles = jnp.floor((total + (MOE_TILE - 1)) * (1.0 / MOE_TILE))
        rr = lax.broadcasted_iota(jnp.int32, (LANES, LANES), 0)
        cc = lax.broadcasted_iota(jnp.int32, (LANES, LANES), 1)
        upper = jnp.where(rr < cc, 1.0, 0.0).astype(BF16)
        first = jnp.dot(jnp.broadcast_to(ktiles, (8, LANES)).astype(BF16), upper,
                        preferred_element_type=F32)[0:1, :]
        base_scr[...] = first * MOE_TILE
        ti = lax.broadcasted_iota(jnp.int32, (ntm, LANES), 0).astype(F32)
        el = lax.broadcasted_iota(jnp.int32, (ntm, LANES), 1).astype(F32)
        owned = jnp.where(ti >= first, jnp.where(ti < first + ktiles, 1.0, 0.0), 0.0)
        texp = jnp.sum(owned * el, axis=1, keepdims=True)
        nact = jnp.sum(ktiles, axis=1, keepdims=True)
        tinfo_ref[...] = jnp.where(el == 0.0, texp, jnp.where(el == 1.0, nact, 0.0))

    rr = lax.broadcasted_iota(jnp.int32, (tm, tm), 0)
    cc = lax.broadcasted_iota(jnp.int32, (tm, tm), 1)
    lower = jnp.where(cc < rr, 1.0, 0.0).astype(BF16)
    before = jnp.dot(lower, both.astype(BF16), preferred_element_type=F32)
    val = before + base_scr[...]
    p0 = jnp.sum(oh0 * val, axis=1, keepdims=True)
    p1 = jnp.sum(oh1 * val, axis=1, keepdims=True)
    pos_ref[...] = jnp.where(lane == 0.0, p0, jnp.where(lane == 1.0, p1, 0.0))
    base_scr[...] += tile_cnt


def _plan(rinfo, cnt, tm=512):
    s = rinfo.shape[0]
    ntm = _moe_num_tiles(s)
    return pl.pallas_call(
        functools.partial(_plan_kernel, tm=tm, ntm=ntm),
        out_shape=(jax.ShapeDtypeStruct((s, LANES), F32), jax.ShapeDtypeStruct((ntm, LANES), F32)),
        grid=(s // tm,),
        in_specs=[pl.BlockSpec((tm, LANES), lambda i: (i, 0)),
                  pl.BlockSpec(cnt.shape, lambda i: (0, 0, 0))],
        out_specs=(pl.BlockSpec((tm, LANES), lambda i: (i, 0)),
                   pl.BlockSpec((ntm, LANES), lambda i: (0, 0))),
        scratch_shapes=[pltpu.VMEM((1, LANES), F32)],
        compiler_params=pltpu.CompilerParams(dimension_semantics=("arbitrary",),
                                             vmem_limit_bytes=VMEM_LIMIT),
        name="plan",
    )(rinfo, cnt)


def _dispatch_kernel(pos_ref, texp_ref, nact_ref, h2_ref, xs_ref, zero_scr, sem, zsem, *, tm):
    i = pl.program_id(0)

    @pl.when(i == 0)
    def _():
        zero_scr[...] = jnp.zeros_like(zero_scr)
        nact = nact_ref[0]
        ntm = xs_ref.shape[0] // MOE_TILE

        def is_last(t):
            nxt = texp_ref[jnp.minimum(t + 1, nact - 1)]
            return (t >= nact - 1) | (nxt != texp_ref[jnp.minimum(t, nact - 1)])

        def zstart(t, carry):
            @pl.when(is_last(t))
            def _():
                pltpu.make_async_copy(zero_scr, xs_ref.at[pl.ds(pl.multiple_of(t * MOE_TILE, MOE_TILE), MOE_TILE), :],
                                      zsem).start()
            return carry

        def zwait(t, carry):
            @pl.when(is_last(t))
            def _():
                pltpu.make_async_copy(zero_scr, xs_ref.at[pl.ds(0, MOE_TILE), :], zsem).wait()
            return carry

        lax.fori_loop(0, ntm, zstart, 0)
        lax.fori_loop(0, ntm, zwait, 0)

    def start(r, carry):
        tok = i * tm + r
        for slot in range(2):
            p = pos_ref[2 * tok + slot]
            pltpu.make_async_copy(h2_ref.at[pl.ds(r, 1), :], xs_ref.at[pl.ds(p, 1), :], sem).start(priority=slot)
        return carry

    lax.fori_loop(0, tm, start, 0, unroll=8)
    for slot in range(2):
        pltpu.make_async_copy(h2_ref, xs_ref.at[pl.ds(0, tm), :], sem).wait()


def _dispatch(pos_flat, texp, nact, h2, tm=1024):
    s, d = h2.shape
    ntm = _moe_num_tiles(s)
    return pl.pallas_call(
        functools.partial(_dispatch_kernel, tm=tm),
        out_shape=jax.ShapeDtypeStruct((ntm * MOE_TILE, d), F32),
        grid_spec=pltpu.PrefetchScalarGridSpec(
            num_scalar_prefetch=3,
            grid=(s // tm,),
            in_specs=[pl.BlockSpec((tm, d), lambda i, *_: (i, 0))],
            out_specs=pl.BlockSpec(memory_space=pl.ANY),
            scratch_shapes=[pltpu.VMEM((MOE_TILE, d), F32), pltpu.SemaphoreType.DMA, pltpu.SemaphoreType.DMA]),
        compiler_params=pltpu.CompilerParams(dimension_semantics=("arbitrary",),
                                             vmem_limit_bytes=VMEM_LIMIT),
        name="dispatch",
    )(pos_flat, texp, nact, h2)


def _experts_kernel(texp_ref, nact_ref, xs_ref, wg_hbm, wu_hbm, wd_hbm, ys_ref,
                    wg_f32, wu_f32, wd_f32, ord_scr, sem):
    nact = nact_ref[0]

    def weight_copies(expert, slot):
        return (pltpu.make_async_copy(wg_hbm.at[expert], wg_f32.at[slot], sem.at[slot]),
                pltpu.make_async_copy(wu_hbm.at[expert], wu_f32.at[slot], sem.at[slot]),
                pltpu.make_async_copy(wd_hbm.at[expert], wd_f32.at[slot], sem.at[slot]))

    def one_tile(i, rows):
        j = jnp.minimum(i, nact - 1)
        e = texp_ref[j]
        fresh = (i == 0) | (e != texp_ref[jnp.maximum(j - 1, 0)])

        @pl.when(i == 0)
        def _():
            ord_scr[0] = 0
            for c in weight_copies(e, 0):
                c.start()

        @pl.when((i < nact) & fresh)
        def _():
            slot = ord_scr[0] % 2
            for c in weight_copies(e, slot):
                c.wait()
            nxt = lax.while_loop(lambda t: (t < nact) & (texp_ref[jnp.minimum(t, nact - 1)] == e),
                                 lambda t: t + 1, i + 1)

            @pl.when(nxt < nact)
            def _():
                for c in weight_copies(texp_ref[nxt], 1 - slot):
                    c.start(priority=1)
            ord_scr[0] = ord_scr[0] + 1

        @pl.when(i < nact)
        def _():
            slot = (ord_scr[0] + 1) % 2
            x = xs_ref[rows, :].astype(BF16)
            hg = jnp.dot(x, wg_f32[slot].astype(BF16), preferred_element_type=F32)
            hu = jnp.dot(x, wu_f32[slot].astype(BF16), preferred_element_type=F32)
            a = (hg * _sigmoid(hg)) * hu
            ys_ref[rows, :] = jnp.dot(a.astype(BF16), wd_f32[slot].astype(BF16), preferred_element_type=F32)

        @pl.when(i >= nact)
        def _():
            ys_ref[rows, :] = jnp.zeros((MOE_TILE, ys_ref.shape[1]), F32)

    for sub in range(EXPERT_TILES_PER_STEP):
        one_tile(pl.program_id(0) * EXPERT_TILES_PER_STEP + sub, slice(sub * MOE_TILE, (sub + 1) * MOE_TILE))


def _experts(texp, nact, xs, w_gate, w_up, w_down):
    p, d = xs.shape
    ntm = p // MOE_TILE
    _, _, f = w_gate.shape

    tps = EXPERT_TILES_PER_STEP
    assert ntm % tps == 0

    def tile(i, texp_ref, nact_ref):
        return (jnp.minimum(i, (nact_ref[0] - 1) // tps), 0)

    hbm = pl.BlockSpec(memory_space=pl.ANY)
    return pl.pallas_call(
        _experts_kernel,
        out_shape=jax.ShapeDtypeStruct((p, d), F32),
        grid_spec=pltpu.PrefetchScalarGridSpec(
            num_scalar_prefetch=2,
            grid=(ntm // tps,),
            in_specs=[pl.BlockSpec((tps * MOE_TILE, d), tile), hbm, hbm, hbm],
            out_specs=pl.BlockSpec((tps * MOE_TILE, d), lambda i, *_: (i, 0)),
            scratch_shapes=[pltpu.VMEM((2, d, f), F32), pltpu.VMEM((2, d, f), F32), pltpu.VMEM((2, f, d), F32),
                            pltpu.SMEM((1,), jnp.int32), pltpu.SemaphoreType.DMA((2,))]),
        compiler_params=pltpu.CompilerParams(dimension_semantics=("arbitrary",),
                                             vmem_limit_bytes=VMEM_LIMIT),
        name="experts",
    )(texp, nact, xs, w_gate, w_up, w_down)


def _combine_kernel(pos_ref, ys_ref, rinfo_ref, x1_ref, gpost_ref, gt_ref, o_ref, ybuf, sem, *, tm):
    i = pl.program_id(0)

    def gather(tile, b):
        def start(r, carry):
            tok = tile * tm + r
            for slot in range(2):
                p = pos_ref[2 * tok + slot]
                pltpu.make_async_copy(ys_ref.at[pl.ds(p, 1), :], ybuf.at[b, slot, pl.ds(r, 1), :],
                                      sem.at[b]).start(priority=slot)
            return carry
        lax.fori_loop(0, tm, start, 0, unroll=8)

    @pl.when(i == 0)
    def _():
        gather(0, 0)

    @pl.when(i + 1 < pl.num_programs(0))
    def _():
        gather(i + 1, (i + 1) % 2)

    b = i % 2
    for slot in range(2):
        pltpu.make_async_copy(ys_ref.at[pl.ds(0, tm), :], ybuf.at[b, slot], sem.at[b]).wait()
    r = rinfo_ref[...]
    y = r[:, 2:3] * ybuf[b, 0] + r[:, 3:4] * ybuf[b, 1]
    o_ref[...] = x1_ref[...] + gt_ref[...] * (_rms(y, NORM_EPS) * gpost_ref[...])


def _combine(pos_flat, ys, rinfo, x1, gpost, gt, tm=512):
    s, d = x1.shape
    row = pl.BlockSpec((1, d), lambda i, *_: (0, 0))
    return pl.pallas_call(
        functools.partial(_combine_kernel, tm=tm),
        out_shape=jax.ShapeDtypeStruct((s, d), F32),
        grid_spec=pltpu.PrefetchScalarGridSpec(
            num_scalar_prefetch=1,
            grid=(s // tm,),
            in_specs=[pl.BlockSpec(memory_space=pl.ANY),
                      pl.BlockSpec((tm, LANES), lambda i, *_: (i, 0)),
                      pl.BlockSpec((tm, d), lambda i, *_: (i, 0)),
                      row, row],
            out_specs=pl.BlockSpec((tm, d), lambda i, *_: (i, 0)),
            scratch_shapes=[pltpu.VMEM((2, 2, tm, d), F32), pltpu.SemaphoreType.DMA((2,))]),
        compiler_params=pltpu.CompilerParams(dimension_semantics=("arbitrary",),
                                             vmem_limit_bytes=VMEM_LIMIT),
        name="combine",
    )(pos_flat, ys, rinfo, x1, gpost, gt)


def _moe(h2, rinfo, cnt, w_gate, w_up, w_down, x1, gpost, gt):
    pos, tinfo = _plan(rinfo, cnt)
    pos_flat = pos[:, 0:2].astype(jnp.int32).reshape(-1)
    texp = tinfo[:, 0].astype(jnp.int32)
    nact = tinfo[0:1, 1].astype(jnp.int32)
    xs = _dispatch(pos_flat, texp, nact, h2)
    ys = _experts(texp, nact, xs, w_gate, w_up, w_down)
    return _combine(pos_flat, ys, rinfo, x1, gpost, gt)


def kernel(x, c, w_ada, b_ada, g_pre1, w_in, lam_q1, lam_k1, lam_q2, lam_k2, g_subln, w_conv, w_out,
           g_post1, g_pre2, w_router_g, b_router_g, w_router_e, b_router_e, w_gate, w_up, w_down, g_post2):
    b, s, d = x.shape
    assert b == 1
    depth = w_ada.shape[0]
    slopes = jnp.asarray([2.0 ** (-8.0 * (i + 1) / N_HEADS) for i in range(N_HEADS)], F32)
    x2 = x.reshape(s, d)
    for l in range(depth):
        lam_init = 0.8 - 0.6 * math.exp(-0.3 * l)
        mod = _ada(c.reshape(d, 1), w_ada[l], b_ada[l].reshape(1, -1))
        sh1, sc1, gt1, sh2, sc2, gt2 = [mod[:, k * d:(k + 1) * d] for k in range(6)]
        proj = _inproj(x2, g_pre1[l].reshape(1, d), sc1, sh1, w_in[l])
        y_attn = _attn(slopes, lam_q1[l].reshape(1, -1), lam_k1[l].reshape(1, -1),
                       lam_q2[l].reshape(1, -1), lam_k2[l].reshape(1, -1), proj,
                       g_subln[l].reshape(1, -1), lam_init)
        wr = jnp.concatenate([w_router_g[l], w_router_e[l].reshape(d, N_EXPERTS),
                              jnp.zeros((d, LANES - N_GROUPS - N_EXPERTS), F32)], axis=1).astype(BF16)
        br = jnp.concatenate([b_router_g[l], b_router_e[l].reshape(N_EXPERTS),
                              jnp.zeros((LANES - N_GROUPS - N_EXPERTS,), F32)]).reshape(1, LANES)
        x1, h2, rinfo, cnt = _outproj(y_attn, proj, x2, w_conv[l], w_out[l].astype(BF16),
                                 g_post1[l].reshape(1, d), gt1, g_pre2[l].reshape(1, d), sc2, sh2, wr, br)
        x2 = _moe(h2, rinfo, cnt, w_gate[l], w_up[l], w_down[l], x1, g_post2[l].reshape(1, d), gt2)
    return x2.reshape(b, s, d)
```
